```python
import jax, jax.numpy as jnp
from jax import lax
import numpy as np

D_MODEL = 1024
BATCH = 4
SEQ = 4096
DEPTH = 4
DEC_BATCH = 32
DEC_SEQ = 8
PAST_LEN = 8192
PAGE_SIZE = 128

N_MIXERS = 3
CHUNK = 128
D_A = 3 * D_MODEL
H_A = 8
G_A = D_A // H_A
CONV_W = 31
H_C = 16
HD_C = D_MODEL // H_C
Q_BLOCK = 128
SB_LOGIT_OFFSET = -6.0
N_MEM = 256
MEM_H = 4
MEM_HD = 128
D_FF = 4 * D_MODEL
EPS = 1e-6
N_A = len(range(0, DEPTH, N_MIXERS))
N_B = len(range(1, DEPTH, N_MIXERS))
N_C = len(range(2, DEPTH, N_MIXERS))

kernel_name = 'hybrid_chunkmlp_conv_stickbreaking_decode_step'


def rmsnorm(x, g):
    xf = x.astype(jnp.float32)
    y = xf * lax.rsqrt(jnp.mean(xf * xf, axis=-1, keepdims=True) + EPS)
    return (y * g.astype(jnp.float32)).astype(x.dtype)


def layernorm(x, g, b):
    xf = x.astype(jnp.float32)
    mu = jnp.mean(xf, axis=-1, keepdims=True)
    xc = xf - mu
    var = jnp.mean(xc * xc, axis=-1, keepdims=True)
    y = xc * lax.rsqrt(var + EPS) * g.astype(jnp.float32) + b.astype(jnp.float32)
    return y.astype(x.dtype)


def swiglu(h, w_gu, w_d):
    gate, up = jnp.split(h @ w_gu, 2, axis=-1)
    return (jax.nn.silu(gate) * up) @ w_d


def chunk_mlp(h, w_in, b_in, ln_g, ln_b, w_s, b_s, w_out):
    bsz, length, _ = h.shape
    z = jax.nn.gelu(h @ w_in + b_in)
    u, v = jnp.split(z, 2, axis=-1)
    v = layernorm(v, ln_g, ln_b)
    n_chunks = -(-length // CHUNK)
    pad = n_chunks * CHUNK - length
    vc = jnp.pad(v, ((0, 0), (0, pad), (0, 0))).reshape(bsz, n_chunks, CHUNK, H_A, G_A)
    causal = jnp.tril(jnp.ones((CHUNK, CHUNK), dtype=bool))
    w_causal = w_s * causal[None].astype(w_s.dtype)
    mixed = jnp.einsum('hts,bcshg->bcthg', w_causal, vc) + b_s.T[None, None, :, :, None]
    mixed = mixed.reshape(bsz, n_chunks * CHUNK, D_A)[:, :length]
    return (u * mixed) @ w_out, v


def conv_module(h, hist, w_pw1, b_pw1, w_dw, b_dw, ln_g, ln_b, w_pw2, b_pw2):
    a, gate = jnp.split(h @ w_pw1 + b_pw1, 2, axis=-1)
    g = a * jax.nn.sigmoid(gate)
    full = jnp.concatenate([hist.astype(g.dtype), g], axis=1)
    y = lax.conv_general_dilated(full, w_dw[:, None, :].astype(g.dtype), window_strides=(1,),
                                 padding='VALID', dimension_numbers=('NWC', 'WIO', 'NWC'),
                                 feature_group_count=D_MODEL) + b_dw
    y = jax.nn.silu(layernorm(y, ln_g, ln_b))
    return y @ w_pw2 + b_pw2, full[:, -(CONV_W - 1):]


def stick_breaking(q, k, v, b_sb, q_pos, k_pos):
    scale = HD_C ** -0.5
    z = jnp.einsum('bqhd,bkhd->bhqk', q.astype(jnp.float32), k.astype(jnp.float32)) * scale
    z = z + b_sb.astype(jnp.float32)[None, :, None, None]
    mask = k_pos[None, :] < q_pos[:, None]
    sp = jnp.where(mask, jax.nn.softplus(z), 0.0)
    later = lax.cumsum(sp, axis=3, reverse=True) - sp
    log_a = jax.nn.log_sigmoid(z) - later
    a = jnp.where(mask, jnp.exp(log_a), 0.0)
    o = jnp.einsum('bhqk,bkhd->bqhd', a, v.astype(jnp.float32))
    return o.astype(q.dtype)


def stick_breaking_prompt(q, k, v, b_sb):
    bsz, length = q.shape[:2]
    nb = length // Q_BLOCK
    qb = q.reshape(bsz, nb, Q_BLOCK, H_C, HD_C).swapaxes(0, 1)
    starts = jnp.arange(nb, dtype=jnp.int32) * Q_BLOCK
    k_pos = jnp.arange(length, dtype=jnp.int32)

    def block(args):
        q_blk, start = args
        return stick_breaking(q_blk, k, v, b_sb, start + jnp.arange(Q_BLOCK, dtype=jnp.int32), k_pos)

    o = lax.map(block, (qb, starts))
    return o.swapaxes(0, 1).reshape(bsz, length, H_C * HD_C)


def mem_kv(mem, g, w_kv):
    bsz, n_mem, _ = mem.shape
    k, v = jnp.split(rmsnorm(mem, g) @ w_kv, 2, axis=-1)
    return k.reshape(bsz, n_mem, MEM_H, MEM_HD), v.reshape(bsz, n_mem, MEM_H, MEM_HD)


def mem_attn(h, k, v, w_q, w_o):
    bsz, length, _ = h.shape
    q = (h @ w_q).reshape(bsz, length, MEM_H, MEM_HD)
    s = jnp.einsum('bqhd,bkhd->bhqk', q.astype(jnp.float32), k.astype(jnp.float32)) * (MEM_HD ** -0.5)
    p = jax.nn.softmax(s, axis=-1)
    o = jnp.einsum('bhqk,bkhd->bqhd', p, v.astype(jnp.float32)).astype(h.dtype)
    return o.reshape(bsz, length, MEM_H * MEM_HD) @ w_o


def setup_inputs(seed: int = 0) -> dict:
    key = jax.random.key(seed)
    keys = iter(jax.random.split(key, 64))

    def nrm(shape, scale):
        return jax.random.normal(next(keys), shape, jnp.float32) * scale

    def gain(shape):
        return 1.0 + nrm(shape, 0.02)

    n_pages = PAST_LEN // PAGE_SIZE
    n_pool = (DEC_BATCH * n_pages * 5) // 4
    page_table = jax.random.permutation(next(keys), n_pool)[:DEC_BATCH * n_pages]
    page_table = page_table.reshape(DEC_BATCH, n_pages).astype(jnp.int32)
    return {
        'x_prompt': nrm((BATCH, SEQ, D_MODEL), 1.0),
        'x_sample': nrm((DEC_BATCH, DEC_SEQ, D_MODEL), 1.0),
        'mem_prompt': nrm((BATCH, N_MEM, D_MODEL), 1.0),
        'cache_conv': nrm((N_B, DEC_BATCH, CONV_W - 1, D_MODEL), 0.5),
        'cache_sb_k': nrm((N_C, n_pool, PAGE_SIZE, H_C, HD_C), 1.0),
        'cache_sb_v': nrm((N_C, n_pool, PAGE_SIZE, H_C, HD_C), 1.0),
        'cache_mem_k': nrm((DEPTH, DEC_BATCH, N_MEM, MEM_H, MEM_HD), 1.0),
        'cache_mem_v': nrm((DEPTH, DEC_BATCH, N_MEM, MEM_H, MEM_HD), 1.0),
        'page_table': page_table,
        'g_ffn1': gain((DEPTH, D_MODEL)),
        'f1_w_gu': nrm((DEPTH, D_MODEL, 2 * D_FF), D_MODEL ** -0.5),
        'f1_w_d': nrm((DEPTH, D_FF, D_MODEL), D_FF ** -0.5),
        'g_mix': gain((DEPTH, D_MODEL)),
        'a_w_in': nrm((N_A, D_MODEL, 2 * D_A), D_MODEL ** -0.5),
        'a_b_in': nrm((N_A, 2 * D_A), 0.02),
        'a_ln_g': gain((N_A, D_A)),
        'a_ln_b': nrm((N_A, D_A), 0.02),
        'a_w_s': nrm((N_A, H_A, CHUNK, CHUNK), CHUNK ** -0.5),
        'a_b_s': gain((N_A, H_A, CHUNK)),
        'a_w_out': nrm((N_A, D_A, D_MODEL), D_A ** -0.5),
        'b_w_pw1': nrm((N_B, D_MODEL, 2 * D_MODEL), D_MODEL ** -0.5),
        'b_b_pw1': nrm((N_B, 2 * D_MODEL), 0.02),
        'b_w_dw': nrm((N_B, CONV_W, D_MODEL), CONV_W ** -0.5),
        'b_b_dw': nrm((N_B, D_MODEL), 0.02),
        'b_ln_g': gain((N_B, D_MODEL)),
        'b_ln_b': nrm((N_B, D_MODEL), 0.02),
        'b_w_pw2': nrm((N_B, D_MODEL, D_MODEL), D_MODEL ** -0.5),
        'b_b_pw2': nrm((N_B, D_MODEL), 0.02),
        'c_w_qkv': nrm((N_C, D_MODEL, 3 * H_C * HD_C), D_MODEL ** -0.5),
        'c_b_sb': SB_LOGIT_OFFSET + nrm((N_C, H_C), 0.1),
        'c_w_o': nrm((N_C, H_C * HD_C, D_MODEL), (H_C * HD_C) ** -0.5),
        'g_mem_q': gain((DEPTH, D_MODEL)),
        'g_mem_kv': gain((DEPTH, D_MODEL)),
        'm_w_q': nrm((DEPTH, D_MODEL, MEM_H * MEM_HD), D_MODEL ** -0.5),
        'm_w_kv': nrm((DEPTH, D_MODEL, 2 * MEM_H * MEM_HD), D_MODEL ** -0.5),
        'm_w_o': nrm((DEPTH, MEM_H * MEM_HD, D_MODEL), (MEM_H * MEM_HD) ** -0.5),
        'g_ffn2': gain((DEPTH, D_MODEL)),
        'f2_w_gu': nrm((DEPTH, D_MODEL, 2 * D_FF), D_MODEL ** -0.5),
        'f2_w_d': nrm((DEPTH, D_FF, D_MODEL), D_FF ** -0.5),
        'g_final': gain((D_MODEL,)),
    }


def reference(x_prompt, x_sample, mem_prompt, cache_conv, cache_sb_k, cache_sb_v, cache_mem_k, cache_mem_v,
              page_table, g_ffn1, f1_w_gu, f1_w_d, g_mix, a_w_in, a_b_in, a_ln_g, a_ln_b, a_w_s, a_b_s, a_w_out,
              b_w_pw1, b_b_pw1, b_w_dw, b_b_dw, b_ln_g, b_ln_b, b_w_pw2, b_b_pw2, c_w_qkv, c_b_sb, c_w_o,
              g_mem_q, g_mem_kv, m_w_q, m_w_kv, m_w_o, g_ffn2, f2_w_gu, f2_w_d, g_final):
    bp, lp, _ = x_prompt.shape
    bs, ls, _ = x_sample.shape
    past_len = page_table.shape[1] * cache_sb_k.shape[2]
    open_p = ((lp - 1) // CHUNK) * CHUNK
    open_s = ((ls - 1) // CHUNK) * CHUNK
    chunk_v_p, chunk_v_s, conv_p, conv_s = [], [], [], []
    k_p, v_p, k_s, v_s, mk_p, mv_p = [], [], [], [], [], []

    yp, ys = x_prompt, x_sample
    for l in range(DEPTH):
        kind = l % N_MIXERS
        i = l // N_MIXERS
        yp = yp + 0.5 * swiglu(rmsnorm(yp, g_ffn1[l]), f1_w_gu[l], f1_w_d[l])
        ys = ys + 0.5 * swiglu(rmsnorm(ys, g_ffn1[l]), f1_w_gu[l], f1_w_d[l])
        hp = rmsnorm(yp, g_mix[l])
        hs = rmsnorm(ys, g_mix[l])
        if kind == 0:
            op, vrows_p = chunk_mlp(hp, a_w_in[i], a_b_in[i], a_ln_g[i], a_ln_b[i], a_w_s[i], a_b_s[i], a_w_out[i])
            os_, vrows_s = chunk_mlp(hs, a_w_in[i], a_b_in[i], a_ln_g[i], a_ln_b[i], a_w_s[i], a_b_s[i], a_w_out[i])
            chunk_v_p.append(vrows_p[:, open_p:])
            chunk_v_s.append(vrows_s[:, open_s:])
        elif kind == 1:
            zeros_hist = jnp.zeros((bp, CONV_W - 1, D_MODEL), hp.dtype)
            op, hist_p = conv_module(hp, zeros_hist, b_w_pw1[i], b_b_pw1[i], b_w_dw[i], b_b_dw[i],
                                     b_ln_g[i], b_ln_b[i], b_w_pw2[i], b_b_pw2[i])
            os_, hist_s = conv_module(hs, cache_conv[i], b_w_pw1[i], b_b_pw1[i], b_w_dw[i], b_b_dw[i],
                                      b_ln_g[i], b_ln_b[i], b_w_pw2[i], b_b_pw2[i])
            conv_p.append(hist_p)
            conv_s.append(hist_s)
        else:
            qp, kp, vp = jnp.split(hp @ c_w_qkv[i], 3, axis=-1)
            qp = qp.reshape(bp, lp, H_C, HD_C)
            kp = kp.reshape(bp, lp, H_C, HD_C)
            vp = vp.reshape(bp, lp, H_C, HD_C)
            op = stick_breaking_prompt(qp, kp, vp, c_b_sb[i]) @ c_w_o[i]
            qs, ks_new, vs_new = jnp.split(hs @ c_w_qkv[i], 3, axis=-1)
            qs = qs.reshape(bs, ls, H_C, HD_C)
            ks_new = ks_new.reshape(bs, ls, H_C, HD_C)
            vs_new = vs_new.reshape(bs, ls, H_C, HD_C)
            past_k = cache_sb_k[i][page_table].reshape(bs, past_len, H_C, HD_C)
            past_v = cache_sb_v[i][page_table].reshape(bs, past_len, H_C, HD_C)
            k_all = jnp.concatenate([past_k.astype(ks_new.dtype), ks_new], axis=1)
            v_all = jnp.concatenate([past_v.astype(vs_new.dtype), vs_new], axis=1)
            k_pos = jnp.arange(past_len + ls, dtype=jnp.int32)
            q_pos = past_len + jnp.arange(ls, dtype=jnp.int32)
            os_ = stick_breaking(qs, k_all, v_all, c_b_sb[i], q_pos, k_pos).reshape(bs, ls, H_C * HD_C) @ c_w_o[i]
            k_p.append(kp)
            v_p.append(vp)
            k_s.append(ks_new)
            v_s.append(vs_new)
        yp = yp + op
        ys = ys + os_
        mkp, mvp = mem_kv(mem_prompt, g_mem_kv[l], m_w_kv[l])
        mk_p.append(mkp)
        mv_p.append(mvp)
        yp = yp + mem_attn(rmsnorm(yp, g_mem_q[l]), mkp, mvp, m_w_q[l], m_w_o[l])
        ys = ys + mem_attn(rmsnorm(ys, g_mem_q[l]), cache_mem_k[l], cache_mem_v[l], m_w_q[l], m_w_o[l])
        yp = yp + 0.5 * swiglu(rmsnorm(yp, g_ffn2[l]), f2_w_gu[l], f2_w_d[l])
        ys = ys + 0.5 * swiglu(rmsnorm(ys, g_ffn2[l]), f2_w_gu[l], f2_w_d[l])

    y_prompt = rmsnorm(yp, g_final)
    y_sample = rmsnorm(ys, g_final)
    state_chunk_v_prompt = jnp.stack(chunk_v_p)
    state_chunk_v_sample = jnp.stack(chunk_v_s)
    state_conv_prompt = jnp.stack(conv_p)
    state_conv_sample = jnp.stack(conv_s)
    k_prompt = jnp.stack(k_p)
    v_prompt = jnp.stack(v_p)
    k_sample = jnp.stack(k_s)
    v_sample = jnp.stack(v_s)
    mem_k_prompt = jnp.stack(mk_p)
    mem_v_prompt = jnp.stack(mv_p)
    return (y_prompt, y_sample, state_chunk_v_prompt, state_chunk_v_sample, state_conv_prompt, state_conv_sample,
            k_prompt, v_prompt, k_sample, v_sample, mem_k_prompt, mem_v_prompt)
```

```python
import functools

import jax
import jax.numpy as jnp
from jax import lax
from jax.experimental import pallas as pl
from jax.experimental.pallas import tpu as pltpu

F32 = jnp.float32
BF16 = jnp.bfloat16
EPS = 1e-6

CHUNK = 128
H_A = 8
CONV_W = 31
H_C = 16
MEM_H = 4
SB_KEY_BLOCK = 256
LANES = 128
SUBLANES = 8
MIB = 1 << 20

_NT = (((1,), (1,)), ((), ()))


def _cparams(n_axes, vmem_mib):
    return pltpu.CompilerParams(dimension_semantics=("arbitrary",) * n_axes, vmem_limit_bytes=vmem_mib * MIB)


def _rms(x, g):
    return x * lax.rsqrt(jnp.mean(x * x, axis=-1, keepdims=True) + EPS) * g


def _layernorm(x, g, b):
    mu = jnp.mean(x, axis=-1, keepdims=True)
    xc = x - mu
    var = jnp.mean(xc * xc, axis=-1, keepdims=True)
    return xc * lax.rsqrt(var + EPS) * g + b


def _dot(a, b):
    return jnp.dot(a, b, preferred_element_type=F32)


def _softplus(z):
    return jnp.maximum(z, 0.0) + jnp.log(1.0 + jnp.exp(-jnp.abs(z)))


def _ffn_body(x_ref, g_ref, wg_ref, wu_ref, wd_ref, *rest, final_norm):
    if final_norm:
        gf_ref, o_ref, h_scr, acc_scr = rest
    else:
        o_ref, h_scr, acc_scr = rest
    j = pl.program_id(1)

    @pl.when(j == 0)
    def _():
        h_scr[...] = _rms(x_ref[...], g_ref[...]).astype(BF16)
        acc_scr[...] = jnp.zeros_like(acc_scr)

    h = h_scr[...]
    gate = _dot(h, wg_ref[...].astype(BF16))
    up = _dot(h, wu_ref[...].astype(BF16))
    a = (gate * jax.nn.sigmoid(gate) * up).astype(BF16)
    acc_scr[...] += _dot(a, wd_ref[...].astype(BF16))

    @pl.when(j == pl.num_programs(1) - 1)
    def _():
        y = x_ref[...] + 0.5 * acc_scr[...]
        if final_norm:
            y = _rms(y, gf_ref[...])
        o_ref[...] = y


def _ffn(x, g, w_gu, w_d, layer, *, tm, tf=512, g_final=None):
    t, d = x.shape
    d_ff = w_d.shape[1]
    nj = d_ff // tf
    in_specs = [
        pl.BlockSpec((tm, d), lambda i, j: (i, 0)),
        pl.BlockSpec((None, 1, d), lambda i, j: (layer, 0, 0)),
        pl.BlockSpec((None, d, tf), lambda i, j: (layer, 0, j)),
        pl.BlockSpec((None, d, tf), lambda i, j: (layer, 0, j + nj)),
        pl.BlockSpec((None, tf, d), lambda i, j: (layer, j, 0)),
    ]
    args = [x, g, w_gu, w_gu, w_d]
    if g_final is not None:
        in_specs.append(pl.BlockSpec((1, d), lambda i, j: (0, 0)))
        args.append(g_final)
    return pl.pallas_call(
        functools.partial(_ffn_body, final_norm=g_final is not None),
        grid=(t // tm, nj),
        in_specs=in_specs,
        out_specs=pl.BlockSpec((tm, d), lambda i, j: (i, 0)),
        out_shape=jax.ShapeDtypeStruct((t, d), F32),
        scratch_shapes=[pltpu.VMEM((tm, d), BF16), pltpu.VMEM((tm, d), F32)],
        compiler_params=_cparams(2, 56),
        name="ffn",
    )(*args)


def _mem_kv_body(m_ref, g_ref, w_ref, kv_ref):
    h = _rms(m_ref[...], g_ref[...]).astype(BF16)
    kv_ref[...] = _dot(h, w_ref[...].astype(BF16))


def _mem_kv(mem, g, w_kv):
    n, d = mem.shape
    depth, _, dkv = w_kv.shape
    return pl.pallas_call(
        _mem_kv_body,
        grid=(depth,),
        in_specs=[
            pl.BlockSpec((n, d), lambda l: (0, 0)),
            pl.BlockSpec((None, 1, d), lambda l: (l, 0, 0)),
            pl.BlockSpec((None, d, dkv), lambda l: (l, 0, 0)),
        ],
        out_specs=pl.BlockSpec((None, n, dkv), lambda l: (l, 0, 0)),
        out_shape=jax.ShapeDtypeStruct((depth, n, dkv), F32),
        compiler_params=_cparams(1, 40),
        name="mem_kv",
    )(mem, g, w_kv)


def _mem_attn_body(x_ref, g_ref, wq_ref, wo_ref, k_ref, v_ref, o_ref):
    x = x_ref[...]
    h = _rms(x, g_ref[...]).astype(BF16)
    hd = k_ref.shape[-1] // MEM_H
    q = (_dot(h, wq_ref[...].astype(BF16)) * (hd ** -0.5)).astype(BF16)
    heads = []
    for hh in range(MEM_H):
        cols = slice(hh * hd, (hh + 1) * hd)
        s = lax.dot_general(q[:, cols], k_ref[:, cols].astype(BF16), _NT, preferred_element_type=F32)
        p = jnp.exp(s - jnp.max(s, axis=-1, keepdims=True))
        o = _dot(p.astype(BF16), v_ref[:, cols].astype(BF16))
        heads.append((o / jnp.sum(p, axis=-1, keepdims=True)).astype(BF16))
    o_ref[...] = x + _dot(jnp.concatenate(heads, axis=-1), wo_ref[...].astype(BF16))


def _mem_attn(x, g, w_q, w_o, k, v, layer, *, tm, kv_layer):
    b, l, d = x.shape
    n_mem, dk = k.shape[-2:]
    kv_spec = pl.BlockSpec((None, None, n_mem, dk), lambda bi, i: (kv_layer, bi, 0, 0))
    return pl.pallas_call(
        _mem_attn_body,
        grid=(b, l // tm),
        in_specs=[
            pl.BlockSpec((None, tm, d), lambda bi, i: (bi, i, 0)),
            pl.BlockSpec((None, 1, d), lambda bi, i: (layer, 0, 0)),
            pl.BlockSpec((None, d, dk), lambda bi, i: (layer, 0, 0)),
            pl.BlockSpec((None, dk, d), lambda bi, i: (layer, 0, 0)),
            kv_spec,
            kv_spec,
        ],
        out_specs=pl.BlockSpec((None, tm, d), lambda bi, i: (bi, i, 0)),
        out_shape=jax.ShapeDtypeStruct((b, l, d), F32),
        compiler_params=_cparams(2, 40),
        name="mem_attn",
    )(x, g, w_q, w_o, k, v)


def _chunk_mlp_body(x_ref, g_ref, win_ref, bin_ref, lng_ref, lnb_ref, ws_ref, bs_ref, wout_ref, o_ref, v_ref, um_scr):
    x = x_ref[...]
    tm = x.shape[0]
    d_a = wout_ref.shape[0]
    grp = d_a // H_A
    h = _rms(x, g_ref[...]).astype(BF16)
    z = jax.nn.gelu(_dot(h, win_ref[...]) + bin_ref[...])
    v = _layernorm(z[:, d_a:], lng_ref[...], lnb_ref[...])
    v_rows = v_ref.shape[0]
    v_ref[...] = v[tm - v_rows:, :]
    vb = v.astype(BF16)
    row = lax.broadcasted_iota(jnp.int32, (CHUNK, CHUNK), 0)
    col = lax.broadcasted_iota(jnp.int32, (CHUNK, CHUNK), 1)
    causal = col <= row
    for hh in range(H_A):
        w_causal = jnp.where(causal, ws_ref[hh], 0.0).astype(BF16)
        cols = slice(hh * grp, (hh + 1) * grp)
        for c in range(tm // CHUNK):
            rows = slice(c * CHUNK, (c + 1) * CHUNK)
            mixed = _dot(w_causal, vb[rows, cols]) + bs_ref[hh]
            um_scr[rows, cols] = (z[rows, cols] * mixed).astype(BF16)
    o_ref[...] = x + _dot(um_scr[...], wout_ref[...])


def _chunk_mlp(x, g, w_in, b_in, ln_g, ln_b, w_s, b_s, w_out, layer, idx, *, tm, v_rows):
    b, l, d = x.shape
    d_a = w_out.shape[1]
    return pl.pallas_call(
        _chunk_mlp_body,
        grid=(b, l // tm),
        in_specs=[
            pl.BlockSpec((None, tm, d), lambda bi, i: (bi, i, 0)),
            pl.BlockSpec((None, 1, d), lambda bi, i: (layer, 0, 0)),
            pl.BlockSpec((None, d, 2 * d_a), lambda bi, i: (idx, 0, 0), pipeline_mode=pl.Buffered(1)),
            pl.BlockSpec((None, 1, 2 * d_a), lambda bi, i: (idx, 0, 0)),
            pl.BlockSpec((None, 1, d_a), lambda bi, i: (idx, 0, 0)),
            pl.BlockSpec((None, 1, d_a), lambda bi, i: (idx, 0, 0)),
            pl.BlockSpec((H_A, CHUNK, CHUNK), lambda bi, i: (0, 0, 0)),
            pl.BlockSpec((H_A, CHUNK, 1), lambda bi, i: (0, 0, 0)),
            pl.BlockSpec((None, d_a, d), lambda bi, i: (idx, 0, 0), pipeline_mode=pl.Buffered(1)),
        ],
        out_specs=[
            pl.BlockSpec((None, tm, d), lambda bi, i: (bi, i, 0)),
            pl.BlockSpec((None, v_rows, d_a), lambda bi, i: (bi, 0, 0)),
        ],
        out_shape=[jax.ShapeDtypeStruct((b, l, d), F32), jax.ShapeDtypeStruct((b, v_rows, d_a), F32)],
        scratch_shapes=[pltpu.VMEM((tm, d_a), BF16)],
        compiler_params=_cparams(2, 60),
        name="chunk_mlp",
    )(x, g, w_in, b_in, ln_g, ln_b, w_s, b_s, w_out)


def _conv_body(x_ref, hist_ref, g_ref, w1_ref, b1_ref, wdw_ref, bdw_ref, lng_ref, lnb_ref, w2_ref, b2_ref,
               o_ref, st_ref, full_scr, *, stride):
    i = pl.program_id(1)
    tm, d = x_ref.shape
    nh = (CONV_W - 1) * stride
    hp = full_scr.shape[0] - tm
    h0 = hp - nh

    @pl.when(i == 0)
    def _():
        full_scr[h0:hp, :] = hist_ref[...]

    x = x_ref[...]
    h = _rms(x, g_ref[...]).astype(BF16)
    ag = _dot(h, w1_ref[...]) + b1_ref[...]
    full_scr[hp:, :] = ag[:, :d] * jax.nn.sigmoid(ag[:, d:])
    y = jnp.zeros((tm, d), F32) + bdw_ref[...]
    for w in range(CONV_W):
        y = y + full_scr[h0 + w * stride:h0 + w * stride + tm, :] * wdw_ref[w:w + 1, :]
    y = _layernorm(y, lng_ref[...], lnb_ref[...])
    y = (y * jax.nn.sigmoid(y)).astype(BF16)
    o_ref[...] = x + _dot(y, w2_ref[...]) + b2_ref[...]
    new_hist = full_scr[tm + h0:tm + hp, :]
    st_ref[...] = new_hist
    full_scr[h0:hp, :] = new_hist


def _conv_module(x, hist, g, w1, b1, wdw, bdw, ln_g, ln_b, w2, b2, layer, idx, *, tm, stride):
    grp, r, d = x.shape
    nh = (CONV_W - 1) * stride
    hp = -(-nh // SUBLANES) * SUBLANES
    vec = lambda n: pl.BlockSpec((None, 1, n), lambda gi, i: (idx, 0, 0))
    return pl.pallas_call(
        functools.partial(_conv_body, stride=stride),
        grid=(grp, r // tm),
        in_specs=[
            pl.BlockSpec((None, tm, d), lambda gi, i: (gi, i, 0)),
            pl.BlockSpec((None, nh, d), lambda gi, i: (gi, 0, 0)),
            pl.BlockSpec((None, 1, d), lambda gi, i: (layer, 0, 0)),
            pl.BlockSpec((None, d, 2 * d), lambda gi, i: (idx, 0, 0)),
            vec(2 * d),
            pl.BlockSpec((None, CONV_W, d), lambda gi, i: (idx, 0, 0)),
            vec(d), vec(d), vec(d),
            pl.BlockSpec((None, d, d), lambda gi, i: (idx, 0, 0)),
            vec(d),
        ],
        out_specs=[
            pl.BlockSpec((None, tm, d), lambda gi, i: (gi, i, 0)),
            pl.BlockSpec((None, nh, d), lambda gi, i: (gi, 0, 0)),
        ],
        out_shape=[jax.ShapeDtypeStruct((grp, r, d), F32), jax.ShapeDtypeStruct((grp, nh, d), F32)],
        scratch_shapes=[pltpu.VMEM((hp + tm, d), F32)],
        compiler_params=_cparams(2, 48),
        name="conv_module",
    )(x, hist, g, w1, b1, wdw, bdw, ln_g, ln_b, w2, b2)


def _qkv_body(x_ref, g_ref, w_ref, q_ref, k_ref, v_ref, kb_ref, vb_ref, *, scale):
    d = x_ref.shape[1]
    h = _rms(x_ref[...], g_ref[...]).astype(BF16)
    qkv = _dot(h, w_ref[...])
    q_ref[...] = (qkv[:, :d] * scale).astype(BF16)
    k = qkv[:, d:2 * d]
    v = qkv[:, 2 * d:]
    k_ref[...] = k
    v_ref[...] = v
    kb_ref[...] = k.astype(BF16)
    vb_ref[...] = v.astype(BF16)


def _qkv_proj(x, g, w_qkv, layer, idx, *, tm):
    t, d = x.shape
    row = lambda: pl.BlockSpec((tm, d), lambda i: (i, 0))
    return pl.pallas_call(
        functools.partial(_qkv_body, scale=(d // H_C) ** -0.5),
        grid=(t // tm,),
        in_specs=[
            row(),
            pl.BlockSpec((None, 1, d), lambda i: (layer, 0, 0)),
            pl.BlockSpec((None, d, 3 * d), lambda i: (idx, 0, 0)),
        ],
        out_specs=[row(), row(), row(), row(), row()],
        out_shape=[jax.ShapeDtypeStruct((t, d), dt) for dt in (BF16, F32, F32, BF16, BF16)],
        compiler_params=_cparams(1, 48),
        name="sb_qkv",
    )(x, g, w_qkv)


def _sb_block(z, r_prev, u_mat, mask):
    kb = z.shape[1]
    sp = _softplus(z)
    if mask is not None:
        sp = jnp.where(mask, sp, 0.0)
    hi = sp.astype(BF16)
    lo = (sp - hi.astype(F32)).astype(BF16)
    later = _dot(hi, u_mat) + _dot(lo, u_mat)
    r_rep = jnp.concatenate([r_prev] * (kb // LANES), axis=1)
    a = jnp.exp(z - sp - later - r_rep)
    if mask is not None:
        a = jnp.where(mask, a, 0.0)
    return a, r_prev + jnp.sum(sp, axis=1, keepdims=True)


def _later_matrix(kb):
    j = lax.broadcasted_iota(jnp.int32, (kb, kb), 0)
    s = lax.broadcasted_iota(jnp.int32, (kb, kb), 1)
    return jnp.where(j > s, 1.0, 0.0).astype(BF16)


def _sb_prompt_body(bias_ref, q_ref, k_ref, v_ref, o_ref, acc_scr, r_scr):
    hp = pl.program_id(1)
    i = pl.program_id(2)
    tq = q_ref.shape[0]
    kb = SB_KEY_BLOCK
    nsub = tq // kb
    half = LANES // 2
    lane = lax.broadcasted_iota(jnp.int32, (1, LANES), 1)
    first = lane < half
    q2 = q_ref[...]
    zq = jnp.zeros_like(q2)
    q_heads = (jnp.where(first, q2, zq), jnp.where(first, zq, q2))
    b_heads = (bias_ref[0, 2 * hp], bias_ref[0, 2 * hp + 1])
    u_mat = _later_matrix(kb)
    acc_scr[...] = jnp.zeros_like(acc_scr)
    r_scr[...] = jnp.zeros_like(r_scr)

    def key_block(k0, row0, masked):
        kblk = k_ref[pl.ds(k0, kb), :]
        vblk = v_ref[pl.ds(k0, kb), :]
        zv = jnp.zeros_like(vblk)
        v_heads = (jnp.where(first, vblk, zv), jnp.where(first, zv, vblk))
        n = tq - row0
        mask = None
        if masked:
            mask = lax.broadcasted_iota(jnp.int32, (n, kb), 1) < lax.broadcasted_iota(jnp.int32, (n, kb), 0)
        contrib = jnp.zeros((n, LANES), F32)
        for hh in range(2):
            z = lax.dot_general(q_heads[hh][row0:, :], kblk, _NT, preferred_element_type=F32) + b_heads[hh]
            a, r_new = _sb_block(z, r_scr[hh, row0:, :], u_mat, mask)
            r_scr[hh, row0:, :] = r_new
            contrib = contrib + _dot(a.astype(BF16), v_heads[hh])
        acc_scr[row0:, :] += contrib

    for c in reversed(range(nsub)):
        key_block(pl.multiple_of(i * tq + c * kb, kb), c * kb, True)

    n_before = i * nsub

    def body(t, carry):
        key_block(pl.multiple_of((n_before - 1 - t) * kb, kb), 0, False)
        return carry

    lax.fori_loop(0, n_before, body, 0)
    o_ref[...] = acc_scr[...].astype(o_ref.dtype)


def _sb_prompt(q, k, v, b_sb, *, tq):
    b, l, d = q.shape
    n_pairs = d // LANES
    return pl.pallas_call(
        _sb_prompt_body,
        grid=(b, n_pairs, l // tq),
        in_specs=[
            pl.BlockSpec(memory_space=pltpu.SMEM),
            pl.BlockSpec((None, tq, LANES), lambda bi, p, i: (bi, i, p)),
            pl.BlockSpec((None, l, LANES), lambda bi, p, i: (bi, 0, p)),
            pl.BlockSpec((None, l, LANES), lambda bi, p, i: (bi, 0, p)),
        ],
        out_specs=pl.BlockSpec((None, tq, LANES), lambda bi, p, i: (bi, i, p)),
        out_shape=jax.ShapeDtypeStruct((b, l, d), BF16),
        scratch_shapes=[pltpu.VMEM((tq, LANES), F32), pltpu.VMEM((2, tq, LANES), F32)],
        compiler_params=_cparams(3, 48),
        name="sb_prompt",
    )(b_sb, q, k, v)


def _sb_sample_body(pt_ref, q_ref, kn_ref, vn_ref, bias_ref, *rest, pages_per_step):
    del pt_ref
    k_refs = rest[:pages_per_step]
    v_refs = rest[pages_per_step:2 * pages_per_step]
    o_ref, qblk_scr, acc_scr, r_scr = rest[2 * pages_per_step:]
    s = pl.program_id(1)
    ls, d = q_ref.shape
    hd = d // H_C
    kb = LANES
    u_mat = _later_matrix(kb)
    row_head = lax.broadcasted_iota(jnp.int32, (kb, d), 0) // ls
    col_head = lax.broadcasted_iota(jnp.int32, (kb, d), 1) // hd
    own = row_head == col_head

    def consume(z, v_nt, v_nn, mask):
        a, r_new = _sb_block(z + bias_ref[...], r_scr[...], u_mat, mask)
        r_scr[...] = r_new
        ab = a.astype(BF16)
        if v_nt is not None:
            acc_scr[...] += lax.dot_general(ab, v_nt, _NT, preferred_element_type=F32)
        else:
            acc_scr[...] += _dot(ab, v_nn)

    @pl.when(s == 0)
    def _():
        q_tiled = jnp.concatenate([q_ref[...]] * H_C, axis=0)
        qblk_scr[...] = jnp.where(own, q_tiled, 0.0).astype(BF16)
        acc_scr[...] = jnp.zeros_like(acc_scr)
        r_scr[...] = jnp.zeros_like(r_scr)
        pad = jnp.zeros((kb - ls, d), F32)
        kn = jnp.concatenate([kn_ref[...], pad], axis=0).astype(BF16)
        vn = jnp.concatenate([vn_ref[...], pad], axis=0).astype(BF16)
        z = lax.dot_general(qblk_scr[...], kn, _NT, preferred_element_type=F32)
        qi = lax.broadcasted_iota(jnp.int32, (kb, kb), 0) % ls
        kj = lax.broadcasted_iota(jnp.int32, (kb, kb), 1)
        consume(z, None, vn, kj < qi)

    for r in range(pages_per_step):
        z = _dot(qblk_scr[...], k_refs[r][...].astype(BF16))
        consume(z, v_refs[r][...].astype(BF16), None, None)

    @pl.when(s == pl.num_programs(1) - 1)
    def _():
        acc = jnp.where(own, acc_scr[...], 0.0)
        out = acc[0:ls, :]
        for hh in range(1, H_C):
            out = out + acc[hh * ls:(hh + 1) * ls, :]
        o_ref[...] = out.astype(o_ref.dtype)


def _sb_sample(q, k_new, v_new, bias_rows, cache_kt, cache_vt, page_table, *, pages_per_step):
    bs, ls, d = q.shape
    n_pages = page_table.shape[1]
    page = cache_kt.shape[-1]
    n_steps = n_pages // pages_per_step
    kb = LANES

    def page_spec(r):
        return pl.BlockSpec((None, d, page), lambda b, s, pt: (pt[b, n_pages - 1 - (s * pages_per_step + r)], 0, 0))

    seq = lambda: pl.BlockSpec((None, ls, d), lambda b, s, pt: (b, 0, 0))
    grid_spec = pltpu.PrefetchScalarGridSpec(
        num_scalar_prefetch=1,
        grid=(bs, n_steps),
        in_specs=[seq(), seq(), seq(), pl.BlockSpec((kb, kb), lambda b, s, pt: (0, 0))]
        + [page_spec(r) for r in range(pages_per_step)] * 2,
        out_specs=seq(),
        scratch_shapes=[pltpu.VMEM((kb, d), BF16), pltpu.VMEM((kb, d), F32), pltpu.VMEM((kb, LANES), F32)],
    )
    return pl.pallas_call(
        functools.partial(_sb_sample_body, pages_per_step=pages_per_step),
        grid_spec=grid_spec,
        out_shape=jax.ShapeDtypeStruct((bs, ls, d), BF16),
        compiler_params=_cparams(2, 48),
        name="sb_sample",
    )(page_table, q, k_new, v_new, bias_rows, *([cache_kt] * pages_per_step), *([cache_vt] * pages_per_step))


def _out_proj_body(x_ref, o_ref_in, w_ref, y_ref):
    y_ref[...] = x_ref[...] + _dot(o_ref_in[...], w_ref[...].astype(BF16))


def _out_proj(x, o, w, idx, *, tm):
    t, d = x.shape
    return pl.pallas_call(
        _out_proj_body,
        grid=(t // tm,),
        in_specs=[
            pl.BlockSpec((tm, d), lambda i: (i, 0)),
            pl.BlockSpec((tm, d), lambda i: (i, 0)),
            pl.BlockSpec((None, d, d), lambda i: (idx, 0, 0)),
        ],
        out_specs=pl.BlockSpec((tm, d), lambda i: (i, 0)),
        out_shape=jax.ShapeDtypeStruct((t, d), F32),
        compiler_params=_cparams(1, 40),
        name="sb_out_proj",
    )(x, o, w)


def kernel(x_prompt, x_sample, mem_prompt, cache_conv, cache_sb_k, cache_sb_v, cache_mem_k, cache_mem_v, page_table, g_ffn1, f1_w_gu, f1_w_d, g_mix, a_w_in, a_b_in, a_ln_g, a_ln_b, a_w_s, a_b_s, a_w_out, b_w_pw1, b_b_pw1, b_w_dw, b_b_dw, b_ln_g, b_ln_b, b_w_pw2, b_b_pw2, c_w_qkv, c_b_sb, c_w_o, g_mem_q, g_mem_kv, m_w_q, m_w_kv, m_w_o, g_ffn2, f2_w_gu, f2_w_d, g_final):
    bp, lp, d = x_prompt.shape
    bs, ls, _ = x_sample.shape
    depth = g_ffn1.shape[0]
    n_mem = mem_prompt.shape[1]
    d_a = a_w_out.shape[1]
    tp, ts = bp * lp, bs * ls
    page = cache_sb_k.shape[2]
    seqs_per_chunk = CHUNK // ls

    row3 = lambda a: a[:, None, :]
    g1, gm, gq, gkv, g2 = (row3(a) for a in (g_ffn1, g_mix, g_mem_q, g_mem_kv, g_ffn2))
    gf = g_final[None, :]

    kv_mem = _mem_kv(mem_prompt.reshape(bp * n_mem, d), gkv, m_w_kv)
    dk = kv_mem.shape[-1] // 2
    mk_p = kv_mem[..., :dk].reshape(depth, bp, n_mem, dk)
    mv_p = kv_mem[..., dk:].reshape(depth, bp, n_mem, dk)
    mk_s = cache_mem_k.reshape(depth, bs, n_mem, dk)
    mv_s = cache_mem_v.reshape(depth, bs, n_mem, dk)

    eye = jnp.eye(seqs_per_chunk, dtype=F32)
    a_w_in_b = a_w_in.astype(BF16)
    a_w_out_b = a_w_out.astype(BF16)
    w1_b = b_w_pw1.astype(BF16)
    w2_b = b_w_pw2.astype(BF16)
    wqkv_b = c_w_qkv.astype(BF16)

    yp = x_prompt.reshape(tp, d)
    ys = x_sample.reshape(ts, d)
    chunk_v_p, chunk_v_s, conv_p, conv_s = [], [], [], []
    k_p, v_p, k_s, v_s = [], [], [], []

    for l in range(depth):
        kind, i = l % 3, l // 3
        yp = _ffn(yp, g1, f1_w_gu, f1_w_d, l, tm=1024)
        ys = _ffn(ys, g1, f1_w_gu, f1_w_d, l, tm=ts)
        if kind == 0:
            bias_p = a_b_s[i][:, :, None]
            yp3, vrows_p = _chunk_mlp(yp.reshape(bp, lp, d), gm, a_w_in_b, row3(a_b_in), row3(a_ln_g), row3(a_ln_b),
                                      a_w_s[i], bias_p, a_w_out_b, l, i, tm=256, v_rows=CHUNK)
            w_s_blk = jnp.einsum('ab,hts->hatbs', eye, a_w_s[i][:, :ls, :ls]).reshape(H_A, CHUNK, CHUNK)
            bias_s = jnp.tile(a_b_s[i][:, :ls], (1, seqs_per_chunk))[:, :, None]
            ys3, vrows_s = _chunk_mlp(ys.reshape(1, ts, d), gm, a_w_in_b, row3(a_b_in), row3(a_ln_g), row3(a_ln_b),
                                      w_s_blk, bias_s, a_w_out_b, l, i, tm=ts, v_rows=ts)
            yp, ys = yp3.reshape(tp, d), ys3.reshape(ts, d)
            chunk_v_p.append(vrows_p)
            chunk_v_s.append(vrows_s.reshape(bs, ls, d_a))
        elif kind == 1:
            conv_args = (gm, w1_b, row3(b_b_pw1), b_w_dw, row3(b_b_dw), row3(b_ln_g), row3(b_ln_b), w2_b, row3(b_b_pw2))
            zeros_hist = jnp.zeros((bp, CONV_W - 1, d), F32)
            yp3, hist_p = _conv_module(yp.reshape(bp, lp, d), zeros_hist, *conv_args, l, i, tm=512, stride=1)
            ys_t = ys.reshape(bs, ls, d).transpose(1, 0, 2).reshape(1, ts, d)
            hist_t = cache_conv[i].transpose(1, 0, 2).reshape(1, (CONV_W - 1) * bs, d)
            ys_t, hist_s = _conv_module(ys_t, hist_t, *conv_args, l, i, tm=ts, stride=bs)
            yp = yp3.reshape(tp, d)
            ys = ys_t.reshape(ls, bs, d).transpose(1, 0, 2).reshape(ts, d)
            conv_p.append(hist_p)
            conv_s.append(hist_s.reshape(CONV_W - 1, bs, d).transpose(1, 0, 2))
        else:
            qp, kp, vp, kpb, vpb = _qkv_proj(yp, gm, wqkv_b, l, i, tm=512)
            shp = (bp, lp, d)
            op = _sb_prompt(qp.reshape(shp), kpb.reshape(shp), vpb.reshape(shp), c_b_sb[i][None, :], tq=1024)
            yp = _out_proj(yp, op.reshape(tp, d), c_w_o, i, tm=1024)
            qs, ks_new, vs_new, _, _ = _qkv_proj(ys, gm, wqkv_b, l, i, tm=ts)
            kt = cache_sb_k[i].transpose(0, 2, 3, 1).reshape(-1, d, page)
            vt = cache_sb_v[i].transpose(0, 2, 3, 1).reshape(-1, d, page)
            bias_rows = jnp.broadcast_to(jnp.repeat(c_b_sb[i], ls)[:, None], (H_C * ls, LANES))
            shs = (bs, ls, d)
            os_ = _sb_sample(qs.reshape(shs).astype(F32), ks_new.reshape(shs), vs_new.reshape(shs), bias_rows,
                             kt, vt, page_table, pages_per_step=4)
            ys = _out_proj(ys, os_.reshape(ts, d), c_w_o, i, tm=ts)
            hd = d // H_C
            k_p.append(kp.reshape(bp, lp, H_C, hd))
            v_p.append(vp.reshape(bp, lp, H_C, hd))
            k_s.append(ks_new.reshape(bs, ls, H_C, hd))
            v_s.append(vs_new.reshape(bs, ls, H_C, hd))
        yp = _mem_attn(yp.reshape(bp, lp, d), gq, m_w_q, m_w_o, mk_p, mv_p, l, tm=512, kv_layer=l).reshape(tp, d)
        ys = _mem_attn(ys.reshape(bs, ls, d), gq, m_w_q, m_w_o, mk_s, mv_s, l, tm=ls, kv_layer=l).reshape(ts, d)
        last = l == depth - 1
        yp = _ffn(yp, g2, f2_w_gu, f2_w_d, l, tm=1024, g_final=gf if last else None)
        ys = _ffn(ys, g2, f2_w_gu, f2_w_d, l, tm=ts, g_final=gf if last else None)

    mem_hd = dk // MEM_H
    return (yp.reshape(bp, lp, d), ys.reshape(bs, ls, d),
            jnp.stack(chunk_v_p), jnp.stack(chunk_v_s), jnp.stack(conv_p), jnp.stack(conv_s),
            jnp.stack(k_p), jnp.stack(v_p), jnp.stack(k_s), jnp.stack(v_s),
            mk_p.reshape(depth, bp, n_mem, MEM_H, mem_hd), mv_p.reshape(depth, bp, n_mem, MEM_H, mem_hd))
```

```python
import functools

import jax
import jax.numpy as jnp
from jax import lax
from jax.experimental import pallas as pl
from jax.experimental.pallas import tpu as pltpu

F32 = jnp.float32
BF16 = jnp.bfloat16
EPS = 1e-6

CHUNK = 128
H_A = 8
CONV_W = 31
H_C = 16
MEM_H = 4
SB_KEY_BLOCK = 256
SB_UNROLL = 2
LOG2E = 1.4426950408889634
LANES = 128
SUBLANES = 8
MIB = 1 << 20

_NT = (((1,), (1,)), ((), ()))


def _cparams(n_axes, vmem_mib):
    return pltpu.CompilerParams(dimension_semantics=("arbitrary",) * n_axes, vmem_limit_bytes=vmem_mib * MIB)


def _rms(x, g):
    return x * lax.rsqrt(jnp.mean(x * x, axis=-1, keepdims=True) + EPS) * g


def _layernorm(x, g, b):
    mu = jnp.mean(x, axis=-1, keepdims=True)
    xc = x - mu
    var = jnp.mean(xc * xc, axis=-1, keepdims=True)
    return xc * lax.rsqrt(var + EPS) * g + b


def _dot(a, b):
    return jnp.dot(a, b, preferred_element_type=F32)


def _softplus(z):
    return jnp.maximum(z, 0.0) + jnp.log(1.0 + jnp.exp(-jnp.abs(z)))


def _ffn_body(x_ref, g_ref, wg_ref, wu_ref, wd_ref, *rest, final_norm):
    if final_norm:
        gf_ref, o_ref, h_scr, acc_scr = rest
    else:
        o_ref, h_scr, acc_scr = rest
    j = pl.program_id(1)

    @pl.when(j == 0)
    def _():
        h_scr[...] = _rms(x_ref[...], g_ref[...]).astype(BF16)
        acc_scr[...] = jnp.zeros_like(acc_scr)

    h = h_scr[...]
    gate = _dot(h, wg_ref[...].astype(BF16))
    up = _dot(h, wu_ref[...].astype(BF16))
    a = (gate * jax.nn.sigmoid(gate) * up).astype(BF16)
    acc_scr[...] += _dot(a, wd_ref[...].astype(BF16))

    @pl.when(j == pl.num_programs(1) - 1)
    def _():
        y = x_ref[...] + 0.5 * acc_scr[...]
        if final_norm:
            y = _rms(y, gf_ref[...])
        o_ref[...] = y


def _ffn(x, g, w_gu, w_d, layer, *, tm, tf=512, g_final=None):
    t, d = x.shape
    d_ff = w_d.shape[1]
    nj = d_ff // tf
    in_specs = [
        pl.BlockSpec((tm, d), lambda i, j: (i, 0)),
        pl.BlockSpec((None, 1, d), lambda i, j: (layer, 0, 0)),
        pl.BlockSpec((None, d, tf), lambda i, j: (layer, 0, j)),
        pl.BlockSpec((None, d, tf), lambda i, j: (layer, 0, j + nj)),
        pl.BlockSpec((None, tf, d), lambda i, j: (layer, j, 0)),
    ]
    args = [x, g, w_gu, w_gu, w_d]
    if g_final is not None:
        in_specs.append(pl.BlockSpec((1, d), lambda i, j: (0, 0)))
        args.append(g_final)
    return pl.pallas_call(
        functools.partial(_ffn_body, final_norm=g_final is not None),
        grid=(t // tm, nj),
        in_specs=in_specs,
        out_specs=pl.BlockSpec((tm, d), lambda i, j: (i, 0)),
        out_shape=jax.ShapeDtypeStruct((t, d), F32),
        scratch_shapes=[pltpu.VMEM((tm, d), BF16), pltpu.VMEM((tm, d), F32)],
        compiler_params=_cparams(2, 56),
        name="ffn",
    )(*args)


def _mem_kv_body(m_ref, g_ref, w_ref, kv_ref):
    h = _rms(m_ref[...], g_ref[...]).astype(BF16)
    kv_ref[...] = _dot(h, w_ref[...].astype(BF16))


def _mem_kv(mem, g, w_kv):
    n, d = mem.shape
    depth, _, dkv = w_kv.shape
    return pl.pallas_call(
        _mem_kv_body,
        grid=(depth,),
        in_specs=[
            pl.BlockSpec((n, d), lambda l: (0, 0)),
            pl.BlockSpec((None, 1, d), lambda l: (l, 0, 0)),
            pl.BlockSpec((None, d, dkv), lambda l: (l, 0, 0)),
        ],
        out_specs=pl.BlockSpec((None, n, dkv), lambda l: (l, 0, 0)),
        out_shape=jax.ShapeDtypeStruct((depth, n, dkv), F32),
        compiler_params=_cparams(1, 40),
        name="mem_kv",
    )(mem, g, w_kv)


def _mem_attn_body(x_ref, g_ref, wq_ref, wo_ref, k_ref, v_ref, o_ref):
    x = x_ref[...]
    h = _rms(x, g_ref[...]).astype(BF16)
    hd = k_ref.shape[-1] // MEM_H
    q = (_dot(h, wq_ref[...].astype(BF16)) * (hd ** -0.5)).astype(BF16)
    heads = []
    for hh in range(MEM_H):
        cols = slice(hh * hd, (hh + 1) * hd)
        s = lax.dot_general(q[:, cols], k_ref[:, cols].astype(BF16), _NT, preferred_element_type=F32)
        p = jnp.exp(s - jnp.max(s, axis=-1, keepdims=True))
        o = _dot(p.astype(BF16), v_ref[:, cols].astype(BF16))
        heads.append((o / jnp.sum(p, axis=-1, keepdims=True)).astype(BF16))
    o_ref[...] = x + _dot(jnp.concatenate(heads, axis=-1), wo_ref[...].astype(BF16))


def _mem_attn(x, g, w_q, w_o, k, v, layer, *, tm, kv_layer):
    b, l, d = x.shape
    n_mem, dk = k.shape[-2:]
    kv_spec = pl.BlockSpec((None, None, n_mem, dk), lambda bi, i: (kv_layer, bi, 0, 0))
    return pl.pallas_call(
        _mem_attn_body,
        grid=(b, l // tm),
        in_specs=[
            pl.BlockSpec((None, tm, d), lambda bi, i: (bi, i, 0)),
            pl.BlockSpec((None, 1, d), lambda bi, i: (layer, 0, 0)),
            pl.BlockSpec((None, d, dk), lambda bi, i: (layer, 0, 0)),
            pl.BlockSpec((None, dk, d), lambda bi, i: (layer, 0, 0)),
            kv_spec,
            kv_spec,
        ],
        out_specs=pl.BlockSpec((None, tm, d), lambda bi, i: (bi, i, 0)),
        out_shape=jax.ShapeDtypeStruct((b, l, d), F32),
        compiler_params=_cparams(2, 40),
        name="mem_attn",
    )(x, g, w_q, w_o, k, v)


def _mem_attn_rows_body(x_ref, g_ref, wq_ref, wo_ref, k_ref, v_ref, o_ref, q_scr, a_scr, *, ls):
    b = pl.program_id(0)
    hd = k_ref.shape[-1]

    @pl.when(b == 0)
    def _():
        h = _rms(x_ref[...], g_ref[...]).astype(BF16)
        q_scr[...] = _dot(h, wq_ref[...].astype(BF16)) * (hd ** -0.5)

    rows = pl.ds(pl.multiple_of(b * ls, ls), ls)
    q_seq = q_scr[rows, :]
    q_rows = jnp.concatenate([q_seq[:, hh * hd:(hh + 1) * hd] for hh in range(MEM_H)], axis=0).astype(BF16)
    s = lax.dot_general(q_rows, k_ref[...].astype(BF16), _NT, preferred_element_type=F32)
    row_head = lax.broadcasted_iota(jnp.int32, s.shape, 0) // ls
    col_head = lax.broadcasted_iota(jnp.int32, s.shape, 1) % MEM_H
    s = jnp.where(row_head == col_head, s, -1e30)
    p = jnp.exp(s - jnp.max(s, axis=-1, keepdims=True))
    o = _dot(p.astype(BF16), v_ref[...].astype(BF16)) / jnp.sum(p, axis=-1, keepdims=True)
    a_scr[rows, :] = jnp.concatenate([o[hh * ls:(hh + 1) * ls, :] for hh in range(MEM_H)], axis=1)

    @pl.when(b == pl.num_programs(0) - 1)
    def _():
        o_ref[...] = x_ref[...] + _dot(a_scr[...].astype(BF16), wo_ref[...].astype(BF16))


def _mem_attn_rows(x, g, w_q, w_o, k, v, layer, *, ls):
    t, d = x.shape
    bs = t // ls
    rows_kv, hd = k.shape[-2:]
    dk = MEM_H * hd
    kv_spec = pl.BlockSpec((None, None, rows_kv, hd), lambda b: (layer, b, 0, 0))
    return pl.pallas_call(
        functools.partial(_mem_attn_rows_body, ls=ls),
        grid=(bs,),
        in_specs=[
            pl.BlockSpec((t, d), lambda b: (0, 0)),
            pl.BlockSpec((None, 1, d), lambda b: (layer, 0, 0)),
            pl.BlockSpec((None, d, dk), lambda b: (layer, 0, 0)),
            pl.BlockSpec((None, dk, d), lambda b: (layer, 0, 0)),
            kv_spec,
            kv_spec,
        ],
        out_specs=pl.BlockSpec((t, d), lambda b: (0, 0)),
        out_shape=jax.ShapeDtypeStruct((t, d), F32),
        scratch_shapes=[pltpu.VMEM((t, dk), F32), pltpu.VMEM((t, dk), F32)],
        compiler_params=_cparams(1, 40),
        name="mem_attn_rows",
    )(x, g, w_q, w_o, k, v)


def _chunk_mlp_body(x_ref, g_ref, win_ref, bin_ref, lng_ref, lnb_ref, ws_ref, bs_ref, wout_ref, o_ref, v_ref,
                    v_scr, vb_scr):
    x = x_ref[...]
    tm = x.shape[0]
    d_a = wout_ref.shape[0]
    grp = d_a // H_A
    slab = 2 * grp
    n_slabs = d_a // slab
    h = _rms(x, g_ref[...]).astype(BF16)

    def gelu_slab(c0):
        return jax.nn.gelu(_dot(h, win_ref[:, c0:c0 + slab]) + bin_ref[:, c0:c0 + slab])

    total = jnp.zeros((tm, 1), F32)
    for c in range(n_slabs):
        zc = gelu_slab(d_a + c * slab)
        v_scr[:, c * slab:(c + 1) * slab] = zc
        total = total + jnp.sum(zc, axis=-1, keepdims=True)
    mu = total * (1.0 / d_a)
    sq = jnp.zeros((tm, 1), F32)
    for c in range(n_slabs):
        xc = v_scr[:, c * slab:(c + 1) * slab] - mu
        sq = sq + jnp.sum(xc * xc, axis=-1, keepdims=True)
    rstd = lax.rsqrt(sq * (1.0 / d_a) + EPS)
    for c in range(n_slabs):
        cols = slice(c * slab, (c + 1) * slab)
        vn = (v_scr[:, cols] - mu) * rstd * lng_ref[:, cols] + lnb_ref[:, cols]
        v_scr[:, cols] = vn
        vb_scr[:, cols] = vn.astype(BF16)
    v_rows = v_ref.shape[0]
    v_ref[...] = v_scr[tm - v_rows:, :]

    row = lax.broadcasted_iota(jnp.int32, (CHUNK, CHUNK), 0)
    col = lax.broadcasted_iota(jnp.int32, (CHUNK, CHUNK), 1)
    causal = col <= row
    y = jnp.zeros(x.shape, F32)
    for c in range(n_slabs):
        uc = gelu_slab(c * slab)
        mixed = []
        for hh in (2 * c, 2 * c + 1):
            w_causal = jnp.where(causal, ws_ref[hh], 0.0).astype(BF16)
            cols = slice(hh * grp, (hh + 1) * grp)
            mixed.append(jnp.concatenate(
                [_dot(w_causal, vb_scr[r * CHUNK:(r + 1) * CHUNK, cols]) + bs_ref[hh] for r in range(tm // CHUNK)],
                axis=0))
        um = (uc * jnp.concatenate(mixed, axis=1)).astype(BF16)
        y = y + _dot(um, wout_ref[c * slab:(c + 1) * slab, :])
    o_ref[...] = x + y


def _chunk_mlp(x, g, w_in, b_in, ln_g, ln_b, w_s, b_s, w_out, layer, idx, *, tm, v_rows):
    b, l, d = x.shape
    d_a = w_out.shape[1]
    return pl.pallas_call(
        _chunk_mlp_body,
        grid=(b, l // tm),
        in_specs=[
            pl.BlockSpec((None, tm, d), lambda bi, i: (bi, i, 0)),
            pl.BlockSpec((None, 1, d), lambda bi, i: (layer, 0, 0)),
            pl.BlockSpec((None, d, 2 * d_a), lambda bi, i: (idx, 0, 0), pipeline_mode=pl.Buffered(1)),
            pl.BlockSpec((None, 1, 2 * d_a), lambda bi, i: (idx, 0, 0)),
            pl.BlockSpec((None, 1, d_a), lambda bi, i: (idx, 0, 0)),
            pl.BlockSpec((None, 1, d_a), lambda bi, i: (idx, 0, 0)),
            pl.BlockSpec((H_A, CHUNK, CHUNK), lambda bi, i: (0, 0, 0)),
            pl.BlockSpec((H_A, CHUNK, 1), lambda bi, i: (0, 0, 0)),
            pl.BlockSpec((None, d_a, d), lambda bi, i: (idx, 0, 0), pipeline_mode=pl.Buffered(1)),
        ],
        out_specs=[
            pl.BlockSpec((None, tm, d), lambda bi, i: (bi, i, 0)),
            pl.BlockSpec((None, v_rows, d_a), lambda bi, i: (bi, 0, 0)),
        ],
        out_shape=[jax.ShapeDtypeStruct((b, l, d), F32), jax.ShapeDtypeStruct((b, v_rows, d_a), F32)],
        scratch_shapes=[pltpu.VMEM((tm, d_a), F32), pltpu.VMEM((tm, d_a), BF16)],
        compiler_params=_cparams(2, 60),
        name="chunk_mlp",
    )(x, g, w_in, b_in, ln_g, ln_b, w_s, b_s, w_out)


def _conv_body(x_ref, hist_ref, g_ref, w1_ref, b1_ref, wdw_ref, bdw_ref, lng_ref, lnb_ref, w2_ref, b2_ref,
               o_ref, st_ref, full_scr, win_scr, *, stride):
    i = pl.program_id(1)
    tm, d = x_ref.shape
    nh = (CONV_W - 1) * stride
    hp = full_scr.shape[0] - tm
    h0 = hp - nh

    @pl.when(i == 0)
    def _():
        full_scr[h0:hp, :] = hist_ref[...]

    x = x_ref[...]
    h = _rms(x, g_ref[...]).astype(BF16)
    ag = _dot(h, w1_ref[...]) + b1_ref[...]
    full_scr[hp:, :] = ag[:, :d] * jax.nn.sigmoid(ag[:, d:])
    y = jnp.zeros((tm, d), F32) + bdw_ref[...]
    starts = [h0 + w * stride for w in range(CONV_W)]
    n_windows = 0
    for phase in range(SUBLANES):
        taps = [w for w in range(CONV_W) if starts[w] % SUBLANES == phase]
        if not taps:
            continue
        base = starts[taps[0]]
        rows = starts[taps[-1]] + tm - base
        if phase == 0:
            window, w0 = full_scr, base
        else:
            window, w0 = win_scr.at[n_windows % 2], 0
            window[0:rows, :] = full_scr[base:base + rows, :]
            n_windows += 1
        for w in taps:
            off = w0 + starts[w] - base
            y = y + window[off:off + tm, :] * wdw_ref[w:w + 1, :]
    y = _layernorm(y, lng_ref[...], lnb_ref[...])
    y = (y * jax.nn.sigmoid(y)).astype(BF16)
    o_ref[...] = x + _dot(y, w2_ref[...]) + b2_ref[...]
    new_hist = full_scr[tm + h0:tm + hp, :]
    st_ref[...] = new_hist
    full_scr[h0:hp, :] = new_hist


def _conv_module(x, hist, g, w1, b1, wdw, bdw, ln_g, ln_b, w2, b2, layer, idx, *, tm, stride):
    grp, r, d = x.shape
    nh = (CONV_W - 1) * stride
    hp = -(-nh // SUBLANES) * SUBLANES
    vec = lambda n: pl.BlockSpec((None, 1, n), lambda gi, i: (idx, 0, 0))
    return pl.pallas_call(
        functools.partial(_conv_body, stride=stride),
        grid=(grp, r // tm),
        in_specs=[
            pl.BlockSpec((None, tm, d), lambda gi, i: (gi, i, 0)),
            pl.BlockSpec((None, nh, d), lambda gi, i: (gi, 0, 0)),
            pl.BlockSpec((None, 1, d), lambda gi, i: (layer, 0, 0)),
            pl.BlockSpec((None, d, 2 * d), lambda gi, i: (idx, 0, 0)),
            vec(2 * d),
            pl.BlockSpec((None, CONV_W, d), lambda gi, i: (idx, 0, 0)),
            vec(d), vec(d), vec(d),
            pl.BlockSpec((None, d, d), lambda gi, i: (idx, 0, 0)),
            vec(d),
        ],
        out_specs=[
            pl.BlockSpec((None, tm, d), lambda gi, i: (gi, i, 0)),
            pl.BlockSpec((None, nh, d), lambda gi, i: (gi, 0, 0)),
        ],
        out_shape=[jax.ShapeDtypeStruct((grp, r, d), F32), jax.ShapeDtypeStruct((grp, nh, d), F32)],
        scratch_shapes=[pltpu.VMEM((hp + tm, d), F32),
                        pltpu.VMEM((2, tm + SUBLANES * ((CONV_W - 1) // SUBLANES), d), F32)],
        compiler_params=_cparams(2, 48),
        name="conv_module",
    )(x, hist, g, w1, b1, wdw, bdw, ln_g, ln_b, w2, b2)


def _qkv_body(x_ref, g_ref, w_ref, q_ref, k_ref, v_ref, kb_ref, vb_ref, *, scale):
    d = x_ref.shape[1]
    h = _rms(x_ref[...], g_ref[...]).astype(BF16)
    qkv = _dot(h, w_ref[...])
    q_ref[...] = (qkv[:, :d] * scale).astype(BF16)
    k = qkv[:, d:2 * d]
    v = qkv[:, 2 * d:]
    k_ref[...] = k
    v_ref[...] = v
    kb_ref[...] = k.astype(BF16)
    vb_ref[...] = v.astype(BF16)


def _qkv_proj(x, g, w_qkv, layer, idx, *, tm):
    t, d = x.shape
    row = lambda: pl.BlockSpec((tm, d), lambda i: (i, 0))
    return pl.pallas_call(
        functools.partial(_qkv_body, scale=(d // H_C) ** -0.5 * LOG2E),
        grid=(t // tm,),
        in_specs=[
            row(),
            pl.BlockSpec((None, 1, d), lambda i: (layer, 0, 0)),
            pl.BlockSpec((None, d, 3 * d), lambda i: (idx, 0, 0)),
        ],
        out_specs=[row(), row(), row(), row(), row()],
        out_shape=[jax.ShapeDtypeStruct((t, d), dt) for dt in (BF16, F32, F32, BF16, BF16)],
        compiler_params=_cparams(1, 48),
        name="sb_qkv",
    )(x, g, w_qkv)


def _sb_softplus_cumsum(z2s, u_mat, mask):
    sps = []
    for z2 in z2s:
        sp = jnp.maximum(z2, 0.0) + jnp.log(1.0 + jnp.exp2(-jnp.abs(z2))) * LOG2E
        sps.append(sp if mask is None else jnp.where(mask, sp, 0.0))
    laters = [_dot(sp.astype(BF16), u_mat) for sp in sps]
    return sps, laters


def _sb_weights(z2s, sps, laters, r_prev, mask):
    weights = []
    r = r_prev
    for z2, sp, later in zip(z2s, sps, laters):
        r_rep = jnp.concatenate([r] * (z2.shape[1] // LANES), axis=1)
        a = jnp.exp2(z2 - sp - later - r_rep)
        weights.append(a if mask is None else jnp.where(mask, a, 0.0))
        r = r + jnp.sum(sp, axis=1, keepdims=True)
    return weights, r


def _later_matrix(kb):
    j = lax.broadcasted_iota(jnp.int32, (kb, kb), 0)
    s = lax.broadcasted_iota(jnp.int32, (kb, kb), 1)
    return jnp.where(j > s, 1.0, 0.0).astype(BF16)


def _sb_prompt_body(bias_ref, q_ref, k_ref, v_ref, o_ref, acc_scr, r_scr):
    hp = pl.program_id(1)
    i = pl.program_id(2)
    tq = q_ref.shape[0]
    kb = SB_KEY_BLOCK
    nsub = tq // kb
    half = LANES // 2
    lane = lax.broadcasted_iota(jnp.int32, (1, LANES), 1)
    first = lane < half
    q2 = q_ref[...]
    zq = jnp.zeros_like(q2)
    q_heads = (jnp.where(first, q2, zq), jnp.where(first, zq, q2))
    b_heads = (bias_ref[0, 2 * hp] * LOG2E, bias_ref[0, 2 * hp + 1] * LOG2E)
    u_mat = _later_matrix(kb)
    acc_scr[...] = jnp.zeros_like(acc_scr)
    r_scr[...] = jnp.zeros_like(r_scr)

    def key_blocks(k0s, row0, masked):
        n = tq - row0
        mask = None
        if masked:
            mask = lax.broadcasted_iota(jnp.int32, (n, kb), 1) < lax.broadcasted_iota(jnp.int32, (n, kb), 0)
        kblks = [k_ref[pl.ds(k0, kb), :] for k0 in k0s]
        vblks = [v_ref[pl.ds(k0, kb), :] for k0 in k0s]
        z2s = [[lax.dot_general(q_heads[hh][row0:, :], kblk, _NT, preferred_element_type=F32) + b_heads[hh]
                for kblk in kblks] for hh in range(2)]
        cums = [_sb_softplus_cumsum(z2s[hh], u_mat, mask) for hh in range(2)]
        contrib = jnp.zeros((n, LANES), F32)
        for hh in range(2):
            weights, r_new = _sb_weights(z2s[hh], *cums[hh], r_scr[hh, row0:, :], mask)
            r_scr[hh, row0:, :] = r_new
            v_cat = jnp.concatenate(vblks, axis=0)
            v_head = jnp.where(first, v_cat, 0) if hh == 0 else jnp.where(first, 0, v_cat)
            contrib = contrib + _dot(jnp.concatenate(weights, axis=1).astype(BF16), v_head)
        acc_scr[row0:, :] += contrib

    for c in reversed(range(nsub)):
        key_blocks([pl.multiple_of(i * tq + c * kb, kb)], c * kb, True)

    n_before = i * nsub

    def body(t, carry):
        k_hi = (n_before - 1 - SB_UNROLL * t) * kb
        key_blocks([pl.multiple_of(k_hi - u * kb, kb) for u in range(SB_UNROLL)], 0, False)
        return carry

    lax.fori_loop(0, n_before // SB_UNROLL, body, 0)
    o_ref[...] = acc_scr[...].astype(o_ref.dtype)


def _sb_prompt(q, k, v, b_sb, *, tq):
    b, l, d = q.shape
    n_pairs = d // LANES
    assert l % tq == 0 and tq % (SB_KEY_BLOCK * SB_UNROLL) == 0
    return pl.pallas_call(
        _sb_prompt_body,
        grid=(b, n_pairs, l // tq),
        in_specs=[
            pl.BlockSpec(memory_space=pltpu.SMEM),
            pl.BlockSpec((None, tq, LANES), lambda bi, p, i: (bi, i, p)),
            pl.BlockSpec((None, l, LANES), lambda bi, p, i: (bi, 0, p)),
            pl.BlockSpec((None, l, LANES), lambda bi, p, i: (bi, 0, p)),
        ],
        out_specs=pl.BlockSpec((None, tq, LANES), lambda bi, p, i: (bi, i, p)),
        out_shape=jax.ShapeDtypeStruct((b, l, d), BF16),
        scratch_shapes=[pltpu.VMEM((tq, LANES), F32), pltpu.VMEM((2, tq, LANES), F32)],
        compiler_params=_cparams(3, 48),
        name="sb_prompt",
    )(b_sb, q, k, v)


def _sb_sample_body(pt_ref, q_ref, kn_ref, vn_ref, bias_ref, *rest, pages_per_step):
    del pt_ref
    k_refs = rest[:pages_per_step]
    v_refs = rest[pages_per_step:2 * pages_per_step]
    o_ref, qblk_scr, acc_scr, r_scr = rest[2 * pages_per_step:]
    s = pl.program_id(1)
    ls, d = q_ref.shape
    hd = d // H_C
    kb = LANES
    u_mat = _later_matrix(kb)
    row_head = lax.broadcasted_iota(jnp.int32, (kb, d), 0) // ls
    col_head = lax.broadcasted_iota(jnp.int32, (kb, d), 1) // hd
    own = row_head == col_head
    bias = bias_ref[...] * LOG2E

    @pl.when(s == 0)
    def _():
        q_tiled = jnp.concatenate([q_ref[...]] * H_C, axis=0)
        qblk = jnp.where(own, q_tiled, 0.0).astype(BF16)
        qblk_scr[...] = qblk
        pad = jnp.zeros((kb - ls, d), F32)
        kn = jnp.concatenate([kn_ref[...], pad], axis=0).astype(BF16)
        vn = jnp.concatenate([vn_ref[...], pad], axis=0).astype(BF16)
        z2 = lax.dot_general(qblk, kn, _NT, preferred_element_type=F32) + bias
        qi = lax.broadcasted_iota(jnp.int32, (kb, kb), 0) % ls
        kj = lax.broadcasted_iota(jnp.int32, (kb, kb), 1)
        mask = kj < qi
        sps, laters = _sb_softplus_cumsum([z2], u_mat, mask)
        weights, r_new = _sb_weights([z2], sps, laters, jnp.zeros((kb, LANES), F32), mask)
        r_scr[...] = r_new
        acc_scr[...] = _dot(weights[0].astype(BF16), vn)

    k_cat = jnp.concatenate([k_refs[p][...].astype(BF16) for p in range(pages_per_step)], axis=1)
    v_cat = jnp.concatenate([v_refs[p][...].astype(BF16) for p in range(pages_per_step)], axis=1)
    z2_all = _dot(qblk_scr[...], k_cat)
    z2s = [z2_all[:, p * kb:(p + 1) * kb] + bias for p in range(pages_per_step)]
    sps, laters = _sb_softplus_cumsum(z2s, u_mat, None)
    weights, r_new = _sb_weights(z2s, sps, laters, r_scr[...], None)
    r_scr[...] = r_new
    acc_scr[...] += lax.dot_general(jnp.concatenate(weights, axis=1).astype(BF16), v_cat, _NT,
                                    preferred_element_type=F32)

    @pl.when(s == pl.num_programs(1) - 1)
    def _():
        acc = jnp.where(own, acc_scr[...], 0.0)
        out = acc[0:ls, :]
        for hh in range(1, H_C):
            out = out + acc[hh * ls:(hh + 1) * ls, :]
        o_ref[...] = out.astype(o_ref.dtype)


def _sb_sample(q, k_new, v_new, bias_rows, cache_kt, cache_vt, page_table, *, pages_per_step):
    bs, ls, d = q.shape
    n_pages = page_table.shape[1]
    page = cache_kt.shape[-1]
    n_steps = n_pages // pages_per_step
    kb = LANES

    def page_spec(r):
        return pl.BlockSpec((None, d, page), lambda b, s, pt: (pt[b, n_pages - 1 - (s * pages_per_step + r)], 0, 0))

    seq = lambda: pl.BlockSpec((None, ls, d), lambda b, s, pt: (b, 0, 0))
    grid_spec = pltpu.PrefetchScalarGridSpec(
        num_scalar_prefetch=1,
        grid=(bs, n_steps),
        in_specs=[seq(), seq(), seq(), pl.BlockSpec((kb, kb), lambda b, s, pt: (0, 0))]
        + [page_spec(r) for r in range(pages_per_step)] * 2,
        out_specs=seq(),
        scratch_shapes=[pltpu.VMEM((kb, d), BF16), pltpu.VMEM((kb, d), F32), pltpu.VMEM((kb, LANES), F32)],
    )
    return pl.pallas_call(
        functools.partial(_sb_sample_body, pages_per_step=pages_per_step),
        grid_spec=grid_spec,
        out_shape=jax.ShapeDtypeStruct((bs, ls, d), BF16),
        compiler_params=_cparams(2, 48),
        name="sb_sample",
    )(page_table, q, k_new, v_new, bias_rows, *([cache_kt] * pages_per_step), *([cache_vt] * pages_per_step))


def _out_proj_body(x_ref, o_ref_in, w_ref, y_ref):
    y_ref[...] = x_ref[...] + _dot(o_ref_in[...], w_ref[...].astype(BF16))


def _out_proj(x, o, w, idx, *, tm):
    t, d = x.shape
    return pl.pallas_call(
        _out_proj_body,
        grid=(t // tm,),
        in_specs=[
            pl.BlockSpec((tm, d), lambda i: (i, 0)),
            pl.BlockSpec((tm, d), lambda i: (i, 0)),
            pl.BlockSpec((None, d, d), lambda i: (idx, 0, 0)),
        ],
        out_specs=pl.BlockSpec((tm, d), lambda i: (i, 0)),
        out_shape=jax.ShapeDtypeStruct((t, d), F32),
        compiler_params=_cparams(1, 40),
        name="sb_out_proj",
    )(x, o, w)


def kernel(x_prompt, x_sample, mem_prompt, cache_conv, cache_sb_k, cache_sb_v, cache_mem_k, cache_mem_v, page_table, g_ffn1, f1_w_gu, f1_w_d, g_mix, a_w_in, a_b_in, a_ln_g, a_ln_b, a_w_s, a_b_s, a_w_out, b_w_pw1, b_b_pw1, b_w_dw, b_b_dw, b_ln_g, b_ln_b, b_w_pw2, b_b_pw2, c_w_qkv, c_b_sb, c_w_o, g_mem_q, g_mem_kv, m_w_q, m_w_kv, m_w_o, g_ffn2, f2_w_gu, f2_w_d, g_final):
    bp, lp, d = x_prompt.shape
    bs, ls, _ = x_sample.shape
    depth = g_ffn1.shape[0]
    n_mem = mem_prompt.shape[1]
    d_a = a_w_out.shape[1]
    tp, ts = bp * lp, bs * ls
    page = cache_sb_k.shape[2]
    seqs_per_chunk = CHUNK // ls

    row3 = lambda a: a[:, None, :]
    g1, gm, gq, gkv, g2 = (row3(a) for a in (g_ffn1, g_mix, g_mem_q, g_mem_kv, g_ffn2))
    gf = g_final[None, :]

    kv_mem = _mem_kv(mem_prompt.reshape(bp * n_mem, d), gkv, m_w_kv)
    dk = kv_mem.shape[-1] // 2
    mk_p = kv_mem[..., :dk].reshape(depth, bp, n_mem, dk)
    mv_p = kv_mem[..., dk:].reshape(depth, bp, n_mem, dk)
    mk_s = cache_mem_k.reshape(depth, bs, n_mem * MEM_H, dk // MEM_H)
    mv_s = cache_mem_v.reshape(depth, bs, n_mem * MEM_H, dk // MEM_H)

    eye = jnp.eye(seqs_per_chunk, dtype=F32)
    a_w_in_b = a_w_in.astype(BF16)
    a_w_out_b = a_w_out.astype(BF16)
    w1_b = b_w_pw1.astype(BF16)
    w2_b = b_w_pw2.astype(BF16)
    wqkv_b = c_w_qkv.astype(BF16)

    yp = x_prompt.reshape(tp, d)
    ys = x_sample.reshape(ts, d)
    chunk_v_p, chunk_v_s, conv_p, conv_s = [], [], [], []
    k_p, v_p, k_s, v_s = [], [], [], []

    for l in range(depth):
        kind, i = l % 3, l // 3
        yp = _ffn(yp, g1, f1_w_gu, f1_w_d, l, tm=1024)
        ys = _ffn(ys, g1, f1_w_gu, f1_w_d, l, tm=ts)
        if kind == 0:
            bias_p = a_b_s[i][:, :, None]
            yp3, vrows_p = _chunk_mlp(yp.reshape(bp, lp, d), gm, a_w_in_b, row3(a_b_in), row3(a_ln_g), row3(a_ln_b),
                                      a_w_s[i], bias_p, a_w_out_b, l, i, tm=512, v_rows=CHUNK)
            w_s_blk = jnp.einsum('ab,hts->hatbs', eye, a_w_s[i][:, :ls, :ls]).reshape(H_A, CHUNK, CHUNK)
            bias_s = jnp.tile(a_b_s[i][:, :ls], (1, seqs_per_chunk))[:, :, None]
            ys3, vrows_s = _chunk_mlp(ys.reshape(1, ts, d), gm, a_w_in_b, row3(a_b_in), row3(a_ln_g), row3(a_ln_b),
                                      w_s_blk, bias_s, a_w_out_b, l, i, tm=ts, v_rows=ts)
            yp, ys = yp3.reshape(tp, d), ys3.reshape(ts, d)
            chunk_v_p.append(vrows_p)
            chunk_v_s.append(vrows_s.reshape(bs, ls, d_a))
        elif kind == 1:
            conv_args = (gm, w1_b, row3(b_b_pw1), b_w_dw, row3(b_b_dw), row3(b_ln_g), row3(b_ln_b), w2_b, row3(b_b_pw2))
            zeros_hist = jnp.zeros((bp, CONV_W - 1, d), F32)
            yp3, hist_p = _conv_module(yp.reshape(bp, lp, d), zeros_hist, *conv_args, l, i, tm=512, stride=1)
            ys_t = ys.reshape(bs, ls, d).transpose(1, 0, 2).reshape(1, ts, d)
            hist_t = cache_conv[i].transpose(1, 0, 2).reshape(1, (CONV_W - 1) * bs, d)
            ys_t, hist_s = _conv_module(ys_t, hist_t, *conv_args, l, i, tm=ts, stride=bs)
            yp = yp3.reshape(tp, d)
            ys = ys_t.reshape(ls, bs, d).transpose(1, 0, 2).reshape(ts, d)
            conv_p.append(hist_p)
            conv_s.append(hist_s.reshape(CONV_W - 1, bs, d).transpose(1, 0, 2))
        else:
            qp, kp, vp, kpb, vpb = _qkv_proj(yp, gm, wqkv_b, l, i, tm=512)
            shp = (bp, lp, d)
            op = _sb_prompt(qp.reshape(shp), kpb.reshape(shp), vpb.reshape(shp), c_b_sb[i][None, :], tq=1024)
            yp = _out_proj(yp, op.reshape(tp, d), c_w_o, i, tm=1024)
            qs, ks_new, vs_new, _, _ = _qkv_proj(ys, gm, wqkv_b, l, i, tm=ts)
            kt = cache_sb_k[i].transpose(0, 2, 3, 1).reshape(-1, d, page)
            vt = cache_sb_v[i].transpose(0, 2, 3, 1).reshape(-1, d, page)
            bias_rows = jnp.broadcast_to(jnp.repeat(c_b_sb[i], ls)[:, None], (H_C * ls, LANES))
            shs = (bs, ls, d)
            os_ = _sb_sample(qs.reshape(shs).astype(F32), ks_new.reshape(shs), vs_new.reshape(shs), bias_rows,
                             kt, vt, page_table, pages_per_step=8)
            ys = _out_proj(ys, os_.reshape(ts, d), c_w_o, i, tm=ts)
            hd = d // H_C
            k_p.append(kp.reshape(bp, lp, H_C, hd))
            v_p.append(vp.reshape(bp, lp, H_C, hd))
            k_s.append(ks_new.reshape(bs, ls, H_C, hd))
            v_s.append(vs_new.reshape(bs, ls, H_C, hd))
        yp = _mem_attn(yp.reshape(bp, lp, d), gq, m_w_q, m_w_o, mk_p, mv_p, l, tm=512, kv_layer=l).reshape(tp, d)
        ys = _mem_attn_rows(ys, gq, m_w_q, m_w_o, mk_s, mv_s, l, ls=ls)
        last = l == depth - 1
        yp = _ffn(yp, g2, f2_w_gu, f2_w_d, l, tm=1024, g_final=gf if last else None)
        ys = _ffn(ys, g2, f2_w_gu, f2_w_d, l, tm=ts, g_final=gf if last else None)

    mem_hd = dk // MEM_H
    return (yp.reshape(bp, lp, d), ys.reshape(bs, ls, d),
            jnp.stack(chunk_v_p), jnp.stack(chunk_v_s), jnp.stack(conv_p), jnp.stack(conv_s),
            jnp.stack(k_p), jnp.stack(v_p), jnp.stack(k_s), jnp.stack(v_s),
            mk_p.reshape(depth, bp, n_mem, MEM_H, mem_hd), mv_p.reshape(depth, bp, n_mem, MEM_H, mem_hd))
```

```python
import functools

import jax
import jax.numpy as jnp
from jax import lax
from jax.experimental import pallas as pl
from jax.experimental.pallas import tpu as pltpu

F32 = jnp.float32
BF16 = jnp.bfloat16
EPS = 1e-6

CHUNK = 128
H_A = 8
CONV_W = 31
H_C = 16
MEM_H = 4
SB_KEY_BLOCK = 256
SB_UNROLL = 4
LOG2E = 1.4426950408889634
LANES = 128
SUBLANES = 8
MIB = 1 << 20

_NT = (((1,), (1,)), ((), ()))


def _cparams(n_axes, vmem_mib):
    return pltpu.CompilerParams(dimension_semantics=("arbitrary",) * n_axes, vmem_limit_bytes=vmem_mib * MIB)


def _rms(x, g):
    return x * lax.rsqrt(jnp.mean(x * x, axis=-1, keepdims=True) + EPS) * g


def _layernorm(x, g, b):
    mu = jnp.mean(x, axis=-1, keepdims=True)
    xc = x - mu
    var = jnp.mean(xc * xc, axis=-1, keepdims=True)
    return xc * lax.rsqrt(var + EPS) * g + b


def _dot(a, b):
    return jnp.dot(a, b, preferred_element_type=F32)


def _softplus(z):
    return jnp.maximum(z, 0.0) + jnp.log(1.0 + jnp.exp(-jnp.abs(z)))


def _ffn_body(x_ref, g_ref, wg_ref, wu_ref, wd_ref, *rest, final_norm):
    if final_norm:
        gf_ref, o_ref, h_scr, acc_scr = rest
    else:
        o_ref, h_scr, acc_scr = rest
    j = pl.program_id(1)

    @pl.when(j == 0)
    def _():
        h_scr[...] = _rms(x_ref[...], g_ref[...]).astype(BF16)
        acc_scr[...] = jnp.zeros_like(acc_scr)

    h = h_scr[...]
    gate = _dot(h, wg_ref[...])
    up = _dot(h, wu_ref[...])
    a = (gate * jax.nn.sigmoid(gate) * up).astype(BF16)
    acc_scr[...] += _dot(a, wd_ref[...])

    @pl.when(j == pl.num_programs(1) - 1)
    def _():
        y = x_ref[...] + 0.5 * acc_scr[...]
        if final_norm:
            y = _rms(y, gf_ref[...])
        o_ref[...] = y


def _ffn(x, g, w_gu, w_d, layer, *, tm, tf=1024, g_final=None):
    t, d = x.shape
    d_ff = w_d.shape[1]
    nj = d_ff // tf
    in_specs = [
        pl.BlockSpec((tm, d), lambda i, j: (i, 0)),
        pl.BlockSpec((None, 1, d), lambda i, j: (layer, 0, 0)),
        pl.BlockSpec((None, d, tf), lambda i, j: (layer, 0, j)),
        pl.BlockSpec((None, d, tf), lambda i, j: (layer, 0, j + nj)),
        pl.BlockSpec((None, tf, d), lambda i, j: (layer, j, 0)),
    ]
    args = [x, g, w_gu, w_gu, w_d]
    if g_final is not None:
        in_specs.append(pl.BlockSpec((1, d), lambda i, j: (0, 0)))
        args.append(g_final)
    return pl.pallas_call(
        functools.partial(_ffn_body, final_norm=g_final is not None),
        grid=(t // tm, nj),
        in_specs=in_specs,
        out_specs=pl.BlockSpec((tm, d), lambda i, j: (i, 0)),
        out_shape=jax.ShapeDtypeStruct((t, d), F32),
        scratch_shapes=[pltpu.VMEM((tm, d), BF16), pltpu.VMEM((tm, d), F32)],
        compiler_params=_cparams(2, 56),
        name="ffn",
    )(*args)


def _mem_kv_body(m_ref, g_ref, w_ref, kv_ref):
    h = _rms(m_ref[...], g_ref[...]).astype(BF16)
    kv_ref[...] = _dot(h, w_ref[...].astype(BF16))


def _mem_kv(mem, g, w_kv):
    n, d = mem.shape
    depth, _, dkv = w_kv.shape
    return pl.pallas_call(
        _mem_kv_body,
        grid=(depth,),
        in_specs=[
            pl.BlockSpec((n, d), lambda l: (0, 0)),
            pl.BlockSpec((None, 1, d), lambda l: (l, 0, 0)),
            pl.BlockSpec((None, d, dkv), lambda l: (l, 0, 0)),
        ],
        out_specs=pl.BlockSpec((None, n, dkv), lambda l: (l, 0, 0)),
        out_shape=jax.ShapeDtypeStruct((depth, n, dkv), F32),
        compiler_params=_cparams(1, 40),
        name="mem_kv",
    )(mem, g, w_kv)


def _mem_attn_body(x_ref, g_ref, wq_ref, wo_ref, k_ref, v_ref, o_ref):
    x = x_ref[...]
    h = _rms(x, g_ref[...]).astype(BF16)
    hd = k_ref.shape[-1] // MEM_H
    q = (_dot(h, wq_ref[...].astype(BF16)) * (hd ** -0.5)).astype(BF16)
    heads = []
    for hh in range(MEM_H):
        cols = slice(hh * hd, (hh + 1) * hd)
        s = lax.dot_general(q[:, cols], k_ref[:, cols].astype(BF16), _NT, preferred_element_type=F32)
        p = jnp.exp(s - jnp.max(s, axis=-1, keepdims=True))
        o = _dot(p.astype(BF16), v_ref[:, cols].astype(BF16))
        heads.append((o / jnp.sum(p, axis=-1, keepdims=True)).astype(BF16))
    o_ref[...] = x + _dot(jnp.concatenate(heads, axis=-1), wo_ref[...].astype(BF16))


def _mem_attn(x, g, w_q, w_o, k, v, layer, *, tm, kv_layer):
    b, l, d = x.shape
    n_mem, dk = k.shape[-2:]
    kv_spec = pl.BlockSpec((None, None, n_mem, dk), lambda bi, i: (kv_layer, bi, 0, 0))
    return pl.pallas_call(
        _mem_attn_body,
        grid=(b, l // tm),
        in_specs=[
            pl.BlockSpec((None, tm, d), lambda bi, i: (bi, i, 0)),
            pl.BlockSpec((None, 1, d), lambda bi, i: (layer, 0, 0)),
            pl.BlockSpec((None, d, dk), lambda bi, i: (layer, 0, 0)),
            pl.BlockSpec((None, dk, d), lambda bi, i: (layer, 0, 0)),
            kv_spec,
            kv_spec,
        ],
        out_specs=pl.BlockSpec((None, tm, d), lambda bi, i: (bi, i, 0)),
        out_shape=jax.ShapeDtypeStruct((b, l, d), F32),
        compiler_params=_cparams(2, 40),
        name="mem_attn",
    )(x, g, w_q, w_o, k, v)


def _mem_attn_rows_body(x_ref, g_ref, wq_ref, wo_ref, k_ref, v_ref, o_ref, q_scr, a_scr, *, ls):
    b = pl.program_id(0)
    hd = k_ref.shape[-1]

    @pl.when(b == 0)
    def _():
        h = _rms(x_ref[...], g_ref[...]).astype(BF16)
        q_scr[...] = _dot(h, wq_ref[...].astype(BF16)) * (hd ** -0.5)

    rows = pl.ds(pl.multiple_of(b * ls, ls), ls)
    q_seq = q_scr[rows, :]
    q_rows = jnp.concatenate([q_seq[:, hh * hd:(hh + 1) * hd] for hh in range(MEM_H)], axis=0).astype(BF16)
    s = lax.dot_general(q_rows, k_ref[...].astype(BF16), _NT, preferred_element_type=F32)
    row_head = lax.broadcasted_iota(jnp.int32, s.shape, 0) // ls
    col_head = lax.broadcasted_iota(jnp.int32, s.shape, 1) % MEM_H
    s = jnp.where(row_head == col_head, s, -1e30)
    p = jnp.exp(s - jnp.max(s, axis=-1, keepdims=True))
    o = _dot(p.astype(BF16), v_ref[...].astype(BF16)) / jnp.sum(p, axis=-1, keepdims=True)
    a_scr[rows, :] = jnp.concatenate([o[hh * ls:(hh + 1) * ls, :] for hh in range(MEM_H)], axis=1)

    @pl.when(b == pl.num_programs(0) - 1)
    def _():
        o_ref[...] = x_ref[...] + _dot(a_scr[...].astype(BF16), wo_ref[...].astype(BF16))


def _mem_attn_rows(x, g, w_q, w_o, k, v, layer, *, ls):
    t, d = x.shape
    bs = t // ls
    rows_kv, hd = k.shape[-2:]
    dk = MEM_H * hd
    kv_spec = pl.BlockSpec((None, None, rows_kv, hd), lambda b: (layer, b, 0, 0))
    return pl.pallas_call(
        functools.partial(_mem_attn_rows_body, ls=ls),
        grid=(bs,),
        in_specs=[
            pl.BlockSpec((t, d), lambda b: (0, 0)),
            pl.BlockSpec((None, 1, d), lambda b: (layer, 0, 0)),
            pl.BlockSpec((None, d, dk), lambda b: (layer, 0, 0)),
            pl.BlockSpec((None, dk, d), lambda b: (layer, 0, 0)),
            kv_spec,
            kv_spec,
        ],
        out_specs=pl.BlockSpec((t, d), lambda b: (0, 0)),
        out_shape=jax.ShapeDtypeStruct((t, d), F32),
        scratch_shapes=[pltpu.VMEM((t, dk), F32), pltpu.VMEM((t, dk), F32)],
        compiler_params=_cparams(1, 40),
        name="mem_attn_rows",
    )(x, g, w_q, w_o, k, v)


def _chunk_mlp_body(x_ref, g_ref, win_ref, bin_ref, lng_ref, lnb_ref, ws_ref, bs_ref, wout_ref, o_ref, v_ref,
                    v_scr, vb_scr):
    x = x_ref[...]
    tm = x.shape[0]
    d_a = wout_ref.shape[0]
    grp = d_a // H_A
    slab = 2 * grp
    n_slabs = d_a // slab
    h = _rms(x, g_ref[...]).astype(BF16)

    def gelu_slab(c0):
        return jax.nn.gelu(_dot(h, win_ref[:, c0:c0 + slab]) + bin_ref[:, c0:c0 + slab])

    total = jnp.zeros((tm, 1), F32)
    for c in range(n_slabs):
        zc = gelu_slab(d_a + c * slab)
        v_scr[:, c * slab:(c + 1) * slab] = zc
        total = total + jnp.sum(zc, axis=-1, keepdims=True)
    mu = total * (1.0 / d_a)
    sq = jnp.zeros((tm, 1), F32)
    for c in range(n_slabs):
        xc = v_scr[:, c * slab:(c + 1) * slab] - mu
        sq = sq + jnp.sum(xc * xc, axis=-1, keepdims=True)
    rstd = lax.rsqrt(sq * (1.0 / d_a) + EPS)
    for c in range(n_slabs):
        cols = slice(c * slab, (c + 1) * slab)
        vn = (v_scr[:, cols] - mu) * rstd * lng_ref[:, cols] + lnb_ref[:, cols]
        v_scr[:, cols] = vn
        vb_scr[:, cols] = vn.astype(BF16)
    v_rows = v_ref.shape[0]
    v_ref[...] = v_scr[tm - v_rows:, :]

    row = lax.broadcasted_iota(jnp.int32, (CHUNK, CHUNK), 0)
    col = lax.broadcasted_iota(jnp.int32, (CHUNK, CHUNK), 1)
    causal = col <= row
    y = jnp.zeros(x.shape, F32)
    for c in range(n_slabs):
        uc = gelu_slab(c * slab)
        mixed = []
        for hh in (2 * c, 2 * c + 1):
            w_causal = jnp.where(causal, ws_ref[hh], 0.0).astype(BF16)
            cols = slice(hh * grp, (hh + 1) * grp)
            mixed.append(jnp.concatenate(
                [_dot(w_causal, vb_scr[r * CHUNK:(r + 1) * CHUNK, cols]) + bs_ref[hh] for r in range(tm // CHUNK)],
                axis=0))
        um = (uc * jnp.concatenate(mixed, axis=1)).astype(BF16)
        y = y + _dot(um, wout_ref[c * slab:(c + 1) * slab, :])
    o_ref[...] = x + y


def _chunk_mlp(x, g, w_in, b_in, ln_g, ln_b, w_s, b_s, w_out, layer, idx, *, tm, v_rows):
    b, l, d = x.shape
    d_a = w_out.shape[1]
    return pl.pallas_call(
        _chunk_mlp_body,
        grid=(b, l // tm),
        in_specs=[
            pl.BlockSpec((None, tm, d), lambda bi, i: (bi, i, 0)),
            pl.BlockSpec((None, 1, d), lambda bi, i: (layer, 0, 0)),
            pl.BlockSpec((None, d, 2 * d_a), lambda bi, i: (idx, 0, 0), pipeline_mode=pl.Buffered(1)),
            pl.BlockSpec((None, 1, 2 * d_a), lambda bi, i: (idx, 0, 0)),
            pl.BlockSpec((None, 1, d_a), lambda bi, i: (idx, 0, 0)),
            pl.BlockSpec((None, 1, d_a), lambda bi, i: (idx, 0, 0)),
            pl.BlockSpec((H_A, CHUNK, CHUNK), lambda bi, i: (0, 0, 0)),
            pl.BlockSpec((H_A, CHUNK, 1), lambda bi, i: (0, 0, 0)),
            pl.BlockSpec((None, d_a, d), lambda bi, i: (idx, 0, 0), pipeline_mode=pl.Buffered(1)),
        ],
        out_specs=[
            pl.BlockSpec((None, tm, d), lambda bi, i: (bi, i, 0)),
            pl.BlockSpec((None, v_rows, d_a), lambda bi, i: (bi, 0, 0)),
        ],
        out_shape=[jax.ShapeDtypeStruct((b, l, d), F32), jax.ShapeDtypeStruct((b, v_rows, d_a), F32)],
        scratch_shapes=[pltpu.VMEM((tm, d_a), F32), pltpu.VMEM((tm, d_a), BF16)],
        compiler_params=_cparams(2, 60),
        name="chunk_mlp",
    )(x, g, w_in, b_in, ln_g, ln_b, w_s, b_s, w_out)


def _conv_body(x_ref, hist_ref, g_ref, w1_ref, b1_ref, wdw_ref, bdw_ref, lng_ref, lnb_ref, w2_ref, b2_ref,
               o_ref, st_ref, full_scr, win_scr, *, stride):
    i = pl.program_id(1)
    tm, d = x_ref.shape
    nh = (CONV_W - 1) * stride
    hp = full_scr.shape[0] - tm
    h0 = hp - nh

    @pl.when(i == 0)
    def _():
        full_scr[h0:hp, :] = hist_ref[...]

    x = x_ref[...]
    h = _rms(x, g_ref[...]).astype(BF16)
    ag = _dot(h, w1_ref[...]) + b1_ref[...]
    full_scr[hp:, :] = ag[:, :d] * jax.nn.sigmoid(ag[:, d:])
    y = jnp.zeros((tm, d), F32) + bdw_ref[...]
    starts = [h0 + w * stride for w in range(CONV_W)]
    n_windows = 0
    for phase in range(SUBLANES):
        taps = [w for w in range(CONV_W) if starts[w] % SUBLANES == phase]
        if not taps:
            continue
        base = starts[taps[0]]
        rows = starts[taps[-1]] + tm - base
        if phase == 0:
            window, w0 = full_scr, base
        else:
            window, w0 = win_scr.at[n_windows % 2], 0
            window[0:rows, :] = full_scr[base:base + rows, :]
            n_windows += 1
        for w in taps:
            off = w0 + starts[w] - base
            y = y + window[off:off + tm, :] * wdw_ref[w:w + 1, :]
    y = _layernorm(y, lng_ref[...], lnb_ref[...])
    y = (y * jax.nn.sigmoid(y)).astype(BF16)
    o_ref[...] = x + _dot(y, w2_ref[...]) + b2_ref[...]
    new_hist = full_scr[tm + h0:tm + hp, :]
    st_ref[...] = new_hist
    full_scr[h0:hp, :] = new_hist


def _conv_module(x, hist, g, w1, b1, wdw, bdw, ln_g, ln_b, w2, b2, layer, idx, *, tm, stride):
    grp, r, d = x.shape
    nh = (CONV_W - 1) * stride
    hp = -(-nh // SUBLANES) * SUBLANES
    vec = lambda n: pl.BlockSpec((None, 1, n), lambda gi, i: (idx, 0, 0))
    return pl.pallas_call(
        functools.partial(_conv_body, stride=stride),
        grid=(grp, r // tm),
        in_specs=[
            pl.BlockSpec((None, tm, d), lambda gi, i: (gi, i, 0)),
            pl.BlockSpec((None, nh, d), lambda gi, i: (gi, 0, 0)),
            pl.BlockSpec((None, 1, d), lambda gi, i: (layer, 0, 0)),
            pl.BlockSpec((None, d, 2 * d), lambda gi, i: (idx, 0, 0)),
            vec(2 * d),
            pl.BlockSpec((None, CONV_W, d), lambda gi, i: (idx, 0, 0)),
            vec(d), vec(d), vec(d),
            pl.BlockSpec((None, d, d), lambda gi, i: (idx, 0, 0)),
            vec(d),
        ],
        out_specs=[
            pl.BlockSpec((None, tm, d), lambda gi, i: (gi, i, 0)),
            pl.BlockSpec((None, nh, d), lambda gi, i: (gi, 0, 0)),
        ],
        out_shape=[jax.ShapeDtypeStruct((grp, r, d), F32), jax.ShapeDtypeStruct((grp, nh, d), F32)],
        scratch_shapes=[pltpu.VMEM((hp + tm, d), F32),
                        pltpu.VMEM((2, tm + SUBLANES * ((CONV_W - 1) // SUBLANES), d), F32)],
        compiler_params=_cparams(2, 48),
        name="conv_module",
    )(x, hist, g, w1, b1, wdw, bdw, ln_g, ln_b, w2, b2)


def _qkv_body(x_ref, g_ref, w_ref, q_ref, k_ref, v_ref, kb_ref, vb_ref, *, scale, transposed):
    d = x_ref.shape[1]
    h = _rms(x_ref[...], g_ref[...]).astype(BF16)
    qkv = _dot(h, w_ref[...])
    q_ref[...] = (qkv[:, :d] * scale).astype(BF16)
    k = qkv[:, d:2 * d]
    v = qkv[:, 2 * d:]
    k_ref[...] = k.T if transposed else k
    v_ref[...] = v.T if transposed else v
    kb_ref[...] = k.astype(BF16)
    vb_ref[...] = v.astype(BF16)


def _qkv_proj(x, g, w_qkv, layer, idx, *, tm, seq_len=None):
    t, d = x.shape
    row = lambda: pl.BlockSpec((tm, d), lambda i: (i, 0))
    if seq_len is None:
        kv_spec, kv_shape = row, (t, d)
    else:
        per_seq = seq_len // tm
        kv_spec = lambda: pl.BlockSpec((None, d, tm), lambda i: (i // per_seq, 0, i % per_seq))
        kv_shape = (t // seq_len, d, seq_len)
    return pl.pallas_call(
        functools.partial(_qkv_body, scale=(d // H_C) ** -0.5 * LOG2E, transposed=seq_len is not None),
        grid=(t // tm,),
        in_specs=[
            row(),
            pl.BlockSpec((None, 1, d), lambda i: (layer, 0, 0)),
            pl.BlockSpec((None, d, 3 * d), lambda i: (idx, 0, 0)),
        ],
        out_specs=[row(), kv_spec(), kv_spec(), row(), row()],
        out_shape=[jax.ShapeDtypeStruct((t, d), BF16), jax.ShapeDtypeStruct(kv_shape, F32),
                   jax.ShapeDtypeStruct(kv_shape, F32), jax.ShapeDtypeStruct((t, d), BF16),
                   jax.ShapeDtypeStruct((t, d), BF16)],
        compiler_params=_cparams(1, 48),
        name="sb_qkv",
    )(x, g, w_qkv)


def _sb_softplus_cumsum(z2s, u_mat, mask):
    sps = []
    for z2 in z2s:
        sp = jnp.maximum(z2, 0.0) + jnp.log(1.0 + jnp.exp2(-jnp.abs(z2))) * LOG2E
        sps.append(sp if mask is None else jnp.where(mask, sp, 0.0))
    laters = [_dot(sp.astype(BF16), u_mat) for sp in sps]
    return sps, laters


def _sb_weights(z2s, sps, laters, r_prev, mask):
    weights = []
    r = r_prev
    for z2, sp, later in zip(z2s, sps, laters):
        r_rep = jnp.concatenate([r] * (z2.shape[1] // LANES), axis=1)
        a = jnp.exp2(z2 - sp - later - r_rep)
        weights.append(a if mask is None else jnp.where(mask, a, 0.0))
        r = r + jnp.sum(sp, axis=1, keepdims=True)
    return weights, r


def _later_matrix(kb):
    j = lax.broadcasted_iota(jnp.int32, (kb, kb), 0)
    s = lax.broadcasted_iota(jnp.int32, (kb, kb), 1)
    return jnp.where(j > s, 1.0, 0.0).astype(BF16)


def _sb_prompt_body(bias_ref, q_ref, k_ref, v_ref, o_ref, acc_scr, r_scr):
    hp = pl.program_id(1)
    i = pl.program_id(2)
    tq = q_ref.shape[0]
    kb = SB_KEY_BLOCK
    nsub = tq // kb
    half = LANES // 2
    lane = lax.broadcasted_iota(jnp.int32, (1, LANES), 1)
    first = lane < half
    q2 = q_ref[...]
    zq = jnp.zeros_like(q2)
    q_heads = (jnp.where(first, q2, zq), jnp.where(first, zq, q2))
    b_heads = (bias_ref[0, 2 * hp] * LOG2E, bias_ref[0, 2 * hp + 1] * LOG2E)
    u_mat = _later_matrix(kb)
    acc_scr[...] = jnp.zeros_like(acc_scr)
    r_scr[...] = jnp.zeros_like(r_scr)

    def key_blocks(k0s, row0, masked):
        n = tq - row0
        mask = None
        if masked:
            mask = lax.broadcasted_iota(jnp.int32, (n, kb), 1) < lax.broadcasted_iota(jnp.int32, (n, kb), 0)
        kblks = [k_ref[pl.ds(k0, kb), :] for k0 in k0s]
        vblks = [v_ref[pl.ds(k0, kb), :] for k0 in k0s]
        z2s = [[lax.dot_general(q_heads[hh][row0:, :], kblk, _NT, preferred_element_type=F32) + b_heads[hh]
                for kblk in kblks] for hh in range(2)]
        cums = [_sb_softplus_cumsum(z2s[hh], u_mat, mask) for hh in range(2)]
        contrib = jnp.zeros((n, LANES), F32)
        for hh in range(2):
            weights, r_new = _sb_weights(z2s[hh], *cums[hh], r_scr[hh, row0:, :], mask)
            r_scr[hh, row0:, :] = r_new
            v_cat = jnp.concatenate(vblks, axis=0)
            v_head = jnp.where(first, v_cat, 0) if hh == 0 else jnp.where(first, 0, v_cat)
            contrib = contrib + _dot(jnp.concatenate(weights, axis=1).astype(BF16), v_head)
        acc_scr[row0:, :] += contrib

    for c in reversed(range(nsub)):
        key_blocks([pl.multiple_of(i * tq + c * kb, kb)], c * kb, True)

    n_before = i * nsub

    def body(t, carry):
        k_hi = (n_before - 1 - SB_UNROLL * t) * kb
        key_blocks([pl.multiple_of(k_hi - u * kb, kb) for u in range(SB_UNROLL)], 0, False)
        return carry

    lax.fori_loop(0, n_before // SB_UNROLL, body, 0)
    o_ref[...] = acc_scr[...].astype(o_ref.dtype)


def _sb_prompt(q, k, v, b_sb, *, tq):
    b, l, d = q.shape
    n_pairs = d // LANES
    assert l % tq == 0 and tq % (SB_KEY_BLOCK * SB_UNROLL) == 0
    return pl.pallas_call(
        _sb_prompt_body,
        grid=(b, n_pairs, l // tq),
        in_specs=[
            pl.BlockSpec(memory_space=pltpu.SMEM),
            pl.BlockSpec((None, tq, LANES), lambda bi, p, i: (bi, i, p)),
            pl.BlockSpec((None, l, LANES), lambda bi, p, i: (bi, 0, p)),
            pl.BlockSpec((None, l, LANES), lambda bi, p, i: (bi, 0, p)),
        ],
        out_specs=pl.BlockSpec((None, tq, LANES), lambda bi, p, i: (bi, i, p)),
        out_shape=jax.ShapeDtypeStruct((b, l, d), BF16),
        scratch_shapes=[pltpu.VMEM((tq, LANES), F32), pltpu.VMEM((2, tq, LANES), F32)],
        compiler_params=_cparams(3, 48),
        name="sb_prompt",
    )(b_sb, q, k, v)


def _sb_sample_body(pt_ref, q_ref, kn_ref, vn_ref, bias_ref, *rest, pages_per_step):
    del pt_ref
    k_refs = rest[:pages_per_step]
    v_refs = rest[pages_per_step:2 * pages_per_step]
    o_ref, qblk_scr, acc_scr, r_scr = rest[2 * pages_per_step:]
    s = pl.program_id(1)
    ls, d = q_ref.shape
    hd = d // H_C
    kb = LANES
    u_mat = _later_matrix(kb)
    row_head = lax.broadcasted_iota(jnp.int32, (kb, d), 0) // ls
    col_head = lax.broadcasted_iota(jnp.int32, (kb, d), 1) // hd
    own = row_head == col_head
    bias = bias_ref[...] * LOG2E

    @pl.when(s == 0)
    def _():
        q_tiled = jnp.concatenate([q_ref[...]] * H_C, axis=0)
        qblk = jnp.where(own, q_tiled, 0.0).astype(BF16)
        qblk_scr[...] = qblk
        pad = jnp.zeros((kb - ls, d), F32)
        kn = jnp.concatenate([kn_ref[...], pad], axis=0).astype(BF16)
        vn = jnp.concatenate([vn_ref[...], pad], axis=0).astype(BF16)
        z2 = lax.dot_general(qblk, kn, _NT, preferred_element_type=F32) + bias
        qi = lax.broadcasted_iota(jnp.int32, (kb, kb), 0) % ls
        kj = lax.broadcasted_iota(jnp.int32, (kb, kb), 1)
        mask = kj < qi
        sps, laters = _sb_softplus_cumsum([z2], u_mat, mask)
        weights, r_new = _sb_weights([z2], sps, laters, jnp.zeros((kb, LANES), F32), mask)
        r_scr[...] = r_new
        acc_scr[...] = _dot(weights[0].astype(BF16), vn)

    k_cat = jnp.concatenate([k_refs[p][...].astype(BF16) for p in range(pages_per_step)], axis=1)
    v_cat = jnp.concatenate([v_refs[p][...].astype(BF16) for p in range(pages_per_step)], axis=1)
    z2_all = _dot(qblk_scr[...], k_cat)
    z2s = [z2_all[:, p * kb:(p + 1) * kb] + bias for p in range(pages_per_step)]
    sps, laters = _sb_softplus_cumsum(z2s, u_mat, None)
    weights, r_new = _sb_weights(z2s, sps, laters, r_scr[...], None)
    r_scr[...] = r_new
    acc_scr[...] += lax.dot_general(jnp.concatenate(weights, axis=1).astype(BF16), v_cat, _NT,
                                    preferred_element_type=F32)

    @pl.when(s == pl.num_programs(1) - 1)
    def _():
        acc = jnp.where(own, acc_scr[...], 0.0)
        out = acc[0:ls, :]
        for hh in range(1, H_C):
            out = out + acc[hh * ls:(hh + 1) * ls, :]
        o_ref[...] = out.astype(o_ref.dtype)


def _sb_sample(q, k_new, v_new, bias_rows, cache_kt, cache_vt, page_table, *, pages_per_step):
    bs, ls, d = q.shape
    n_pages = page_table.shape[1]
    page = cache_kt.shape[-1]
    n_steps = n_pages // pages_per_step
    kb = LANES

    def page_spec(r):
        return pl.BlockSpec((None, d, page), lambda b, s, pt: (pt[b, n_pages - 1 - (s * pages_per_step + r)], 0, 0))

    seq = lambda: pl.BlockSpec((None, ls, d), lambda b, s, pt: (b, 0, 0))
    grid_spec = pltpu.PrefetchScalarGridSpec(
        num_scalar_prefetch=1,
        grid=(bs, n_steps),
        in_specs=[seq(), seq(), seq(), pl.BlockSpec((kb, kb), lambda b, s, pt: (0, 0))]
        + [page_spec(r) for r in range(pages_per_step)] * 2,
        out_specs=seq(),
        scratch_shapes=[pltpu.VMEM((kb, d), BF16), pltpu.VMEM((kb, d), F32), pltpu.VMEM((kb, LANES), F32)],
    )
    return pl.pallas_call(
        functools.partial(_sb_sample_body, pages_per_step=pages_per_step),
        grid_spec=grid_spec,
        out_shape=jax.ShapeDtypeStruct((bs, ls, d), BF16),
        compiler_params=_cparams(2, 48),
        name="sb_sample",
    )(page_table, q, k_new, v_new, bias_rows, *([cache_kt] * pages_per_step), *([cache_vt] * pages_per_step))


def _out_proj_body(x_ref, o_ref_in, w_ref, y_ref):
    y_ref[...] = x_ref[...] + _dot(o_ref_in[...], w_ref[...].astype(BF16))


def _out_proj(x, o, w, idx, *, tm):
    t, d = x.shape
    return pl.pallas_call(
        _out_proj_body,
        grid=(t // tm,),
        in_specs=[
            pl.BlockSpec((tm, d), lambda i: (i, 0)),
            pl.BlockSpec((tm, d), lambda i: (i, 0)),
            pl.BlockSpec((None, d, d), lambda i: (idx, 0, 0)),
        ],
        out_specs=pl.BlockSpec((tm, d), lambda i: (i, 0)),
        out_shape=jax.ShapeDtypeStruct((t, d), F32),
        compiler_params=_cparams(1, 40),
        name="sb_out_proj",
    )(x, o, w)


def kernel(x_prompt, x_sample, mem_prompt, cache_conv, cache_sb_k, cache_sb_v, cache_mem_k, cache_mem_v, page_table, g_ffn1, f1_w_gu, f1_w_d, g_mix, a_w_in, a_b_in, a_ln_g, a_ln_b, a_w_s, a_b_s, a_w_out, b_w_pw1, b_b_pw1, b_w_dw, b_b_dw, b_ln_g, b_ln_b, b_w_pw2, b_b_pw2, c_w_qkv, c_b_sb, c_w_o, g_mem_q, g_mem_kv, m_w_q, m_w_kv, m_w_o, g_ffn2, f2_w_gu, f2_w_d, g_final):
    bp, lp, d = x_prompt.shape
    bs, ls, _ = x_sample.shape
    depth = g_ffn1.shape[0]
    n_mem = mem_prompt.shape[1]
    d_a = a_w_out.shape[1]
    tp, ts = bp * lp, bs * ls
    page = cache_sb_k.shape[2]
    seqs_per_chunk = CHUNK // ls

    row3 = lambda a: a[:, None, :]
    g1, gm, gq, gkv, g2 = (row3(a) for a in (g_ffn1, g_mix, g_mem_q, g_mem_kv, g_ffn2))
    gf = g_final[None, :]

    kv_mem = _mem_kv(mem_prompt.reshape(bp * n_mem, d), gkv, m_w_kv)
    dk = kv_mem.shape[-1] // 2
    mk_p = kv_mem[..., :dk].reshape(depth, bp, n_mem, dk)
    mv_p = kv_mem[..., dk:].reshape(depth, bp, n_mem, dk)
    mk_s = cache_mem_k.reshape(depth, bs, n_mem * MEM_H, dk // MEM_H)
    mv_s = cache_mem_v.reshape(depth, bs, n_mem * MEM_H, dk // MEM_H)

    eye = jnp.eye(seqs_per_chunk, dtype=F32)
    f1_w_gu, f1_w_d, f2_w_gu, f2_w_d = (w.astype(BF16) for w in (f1_w_gu, f1_w_d, f2_w_gu, f2_w_d))
    a_w_in_b = a_w_in.astype(BF16)
    a_w_out_b = a_w_out.astype(BF16)
    w1_b = b_w_pw1.astype(BF16)
    w2_b = b_w_pw2.astype(BF16)
    wqkv_b = c_w_qkv.astype(BF16)

    yp = x_prompt.reshape(tp, d)
    ys = x_sample.reshape(ts, d)
    chunk_v_p, chunk_v_s, conv_p, conv_s = [], [], [], []
    k_p, v_p, k_s, v_s = [], [], [], []

    for l in range(depth):
        kind, i = l % 3, l // 3
        yp = _ffn(yp, g1, f1_w_gu, f1_w_d, l, tm=1024)
        ys = _ffn(ys, g1, f1_w_gu, f1_w_d, l, tm=ts)
        if kind == 0:
            bias_p = a_b_s[i][:, :, None]
            yp3, vrows_p = _chunk_mlp(yp.reshape(bp, lp, d), gm, a_w_in_b, row3(a_b_in), row3(a_ln_g), row3(a_ln_b),
                                      a_w_s[i], bias_p, a_w_out_b, l, i, tm=512, v_rows=CHUNK)
            w_s_blk = jnp.einsum('ab,hts->hatbs', eye, a_w_s[i][:, :ls, :ls]).reshape(H_A, CHUNK, CHUNK)
            bias_s = jnp.tile(a_b_s[i][:, :ls], (1, seqs_per_chunk))[:, :, None]
            ys3, vrows_s = _chunk_mlp(ys.reshape(1, ts, d), gm, a_w_in_b, row3(a_b_in), row3(a_ln_g), row3(a_ln_b),
                                      w_s_blk, bias_s, a_w_out_b, l, i, tm=ts, v_rows=ts)
            yp, ys = yp3.reshape(tp, d), ys3.reshape(ts, d)
            chunk_v_p.append(vrows_p)
            chunk_v_s.append(vrows_s.reshape(bs, ls, d_a))
        elif kind == 1:
            conv_args = (gm, w1_b, row3(b_b_pw1), b_w_dw, row3(b_b_dw), row3(b_ln_g), row3(b_ln_b), w2_b, row3(b_b_pw2))
            zeros_hist = jnp.zeros((bp, CONV_W - 1, d), F32)
            yp3, hist_p = _conv_module(yp.reshape(bp, lp, d), zeros_hist, *conv_args, l, i, tm=512, stride=1)
            ys_t = ys.reshape(bs, ls, d).transpose(1, 0, 2).reshape(1, ts, d)
            hist_t = cache_conv[i].transpose(1, 0, 2).reshape(1, (CONV_W - 1) * bs, d)
            ys_t, hist_s = _conv_module(ys_t, hist_t, *conv_args, l, i, tm=ts, stride=bs)
            yp = yp3.reshape(tp, d)
            ys = ys_t.reshape(ls, bs, d).transpose(1, 0, 2).reshape(ts, d)
            conv_p.append(hist_p)
            conv_s.append(hist_s.reshape(CONV_W - 1, bs, d).transpose(1, 0, 2))
        else:
            qp, kp_t, vp_t, kpb, vpb = _qkv_proj(yp, gm, wqkv_b, l, i, tm=512, seq_len=lp)
            shp = (bp, lp, d)
            op = _sb_prompt(qp.reshape(shp), kpb.reshape(shp), vpb.reshape(shp), c_b_sb[i][None, :], tq=1024)
            yp = _out_proj(yp, op.reshape(tp, d), c_w_o, i, tm=1024)
            qs, ks_new, vs_new, _, _ = _qkv_proj(ys, gm, wqkv_b, l, i, tm=ts)
            kt = cache_sb_k[i].transpose(0, 2, 3, 1).reshape(-1, d, page)
            vt = cache_sb_v[i].transpose(0, 2, 3, 1).reshape(-1, d, page)
            bias_rows = jnp.broadcast_to(jnp.repeat(c_b_sb[i], ls)[:, None], (H_C * ls, LANES))
            shs = (bs, ls, d)
            os_ = _sb_sample(qs.reshape(shs).astype(F32), ks_new.reshape(shs), vs_new.reshape(shs), bias_rows,
                             kt, vt, page_table, pages_per_step=8)
            ys = _out_proj(ys, os_.reshape(ts, d), c_w_o, i, tm=ts)
            hd = d // H_C
            k_p.append(kp_t.reshape(bp, H_C, hd, lp).transpose(0, 3, 1, 2))
            v_p.append(vp_t.reshape(bp, H_C, hd, lp).transpose(0, 3, 1, 2))
            k_s.append(ks_new.reshape(bs, ls, H_C, hd))
            v_s.append(vs_new.reshape(bs, ls, H_C, hd))
        yp = _mem_attn(yp.reshape(bp, lp, d), gq, m_w_q, m_w_o, mk_p, mv_p, l, tm=512, kv_layer=l).reshape(tp, d)
        ys = _mem_attn_rows(ys, gq, m_w_q, m_w_o, mk_s, mv_s, l, ls=ls)
        last = l == depth - 1
        yp = _ffn(yp, g2, f2_w_gu, f2_w_d, l, tm=1024, g_final=gf if last else None)
        ys = _ffn(ys, g2, f2_w_gu, f2_w_d, l, tm=ts, g_final=gf if last else None)

    mem_hd = dk // MEM_H
    return (yp.reshape(bp, lp, d), ys.reshape(bs, ls, d),
            jnp.stack(chunk_v_p), jnp.stack(chunk_v_s), jnp.stack(conv_p), jnp.stack(conv_s),
            jnp.stack(k_p), jnp.stack(v_p), jnp.stack(k_s), jnp.stack(v_s),
            mk_p.reshape(depth, bp, n_mem, MEM_H, mem_hd), mv_p.reshape(depth, bp, n_mem, MEM_H, mem_hd))
```

```python
import functools

import jax
import jax.numpy as jnp
from jax import lax
from jax.experimental import pallas as pl
from jax.experimental.pallas import tpu as pltpu

F32 = jnp.float32
BF16 = jnp.bfloat16
EPS = 1e-6

CHUNK = 128
H_A = 8
CONV_W = 31
H_C = 16
MEM_H = 4
SB_KEY_BLOCK = 256
SB_UNROLL = 4
LOG2E = 1.4426950408889634
LANES = 128
SUBLANES = 8
MIB = 1 << 20

_NT = (((1,), (1,)), ((), ()))


def _cparams(n_axes, vmem_mib):
    return pltpu.CompilerParams(dimension_semantics=("arbitrary",) * n_axes, vmem_limit_bytes=vmem_mib * MIB)


def _rms(x, g):
    return x * lax.rsqrt(jnp.mean(x * x, axis=-1, keepdims=True) + EPS) * g


def _layernorm(x, g, b):
    mu = jnp.mean(x, axis=-1, keepdims=True)
    xc = x - mu
    var = jnp.mean(xc * xc, axis=-1, keepdims=True)
    return xc * lax.rsqrt(var + EPS) * g + b


def _dot(a, b):
    return jnp.dot(a, b, preferred_element_type=F32)


def _softplus(z):
    return jnp.maximum(z, 0.0) + jnp.log(1.0 + jnp.exp(-jnp.abs(z)))


def _ffn_body(x_ref, g_ref, wg_ref, wu_ref, wd_ref, *rest, final_norm):
    gf_ref = rest[0] if final_norm else None
    o_ref, h_scr, acc_scr = rest[-3:]
    j = pl.program_id(1)

    @pl.when(j == 0)
    def _():
        h_scr[...] = _rms(x_ref[...], g_ref[...]).astype(BF16)
        acc_scr[...] = jnp.zeros_like(acc_scr)

    h = h_scr[...]
    gate = _dot(h, wg_ref[...])
    up = _dot(h, wu_ref[...])
    a = (gate * jax.nn.sigmoid(gate) * up).astype(BF16)
    acc_scr[...] += _dot(a, wd_ref[...])

    @pl.when(j == pl.num_programs(1) - 1)
    def _():
        y = x_ref[...] + 0.5 * acc_scr[...]
        if final_norm:
            y = _rms(y, gf_ref[...])
        o_ref[...] = y


def _ffn(x, g, w_g, w_u, w_d, layer, *, tm, tf=1024, g_final=None):
    t, d = x.shape
    d_ff = w_d.shape[0]
    nj = d_ff // tf
    in_specs = [
        pl.BlockSpec((tm, d), lambda i, j: (i, 0)),
        pl.BlockSpec((None, 1, d), lambda i, j: (layer, 0, 0)),
        pl.BlockSpec((d, tf), lambda i, j: (0, j)),
        pl.BlockSpec((d, tf), lambda i, j: (0, j)),
        pl.BlockSpec((tf, d), lambda i, j: (j, 0)),
    ]
    args = [x, g, w_g, w_u, w_d]
    if g_final is not None:
        in_specs.append(pl.BlockSpec((1, d), lambda i, j: (0, 0)))
        args.append(g_final)
    return pl.pallas_call(
        functools.partial(_ffn_body, final_norm=g_final is not None),
        grid=(t // tm, nj),
        in_specs=in_specs,
        out_specs=pl.BlockSpec((tm, d), lambda i, j: (i, 0)),
        out_shape=jax.ShapeDtypeStruct((t, d), F32),
        scratch_shapes=[pltpu.VMEM((tm, d), BF16), pltpu.VMEM((tm, d), F32)],
        compiler_params=_cparams(2, 56),
        name="ffn",
    )(*args)


def _ffn_cast_body(x_ref, g_ref, wg_ref, wu_ref, wd_ref, *rest, final_norm):
    gf_ref = rest[0] if final_norm else None
    o_ref, wg_out, wu_out, wd_out, h_scr, acc_scr = rest[-6:]
    j = pl.program_id(0)

    @pl.when(j == 0)
    def _():
        h_scr[...] = _rms(x_ref[...], g_ref[...]).astype(BF16)
        acc_scr[...] = jnp.zeros_like(acc_scr)

    wg = wg_ref[...].astype(BF16)
    wu = wu_ref[...].astype(BF16)
    wd = wd_ref[...].astype(BF16)
    wg_out[...] = wg
    wu_out[...] = wu
    wd_out[...] = wd
    h = h_scr[...]
    gate = _dot(h, wg)
    up = _dot(h, wu)
    a = (gate * jax.nn.sigmoid(gate) * up).astype(BF16)
    acc_scr[...] += _dot(a, wd)

    @pl.when(j == pl.num_programs(0) - 1)
    def _():
        y = x_ref[...] + 0.5 * acc_scr[...]
        if final_norm:
            y = _rms(y, gf_ref[...])
        o_ref[...] = y


def _ffn_cast(x, g, w_gu, w_d, layer, *, tf=512, g_final=None):
    t, d = x.shape
    d_ff = w_d.shape[1]
    nj = d_ff // tf
    in_specs = [
        pl.BlockSpec((t, d), lambda j: (0, 0)),
        pl.BlockSpec((None, 1, d), lambda j: (layer, 0, 0)),
        pl.BlockSpec((None, d, tf), lambda j: (layer, 0, j)),
        pl.BlockSpec((None, d, tf), lambda j: (layer, 0, j + nj)),
        pl.BlockSpec((None, tf, d), lambda j: (layer, j, 0)),
    ]
    args = [x, g, w_gu, w_gu, w_d]
    if g_final is not None:
        in_specs.append(pl.BlockSpec((1, d), lambda j: (0, 0)))
        args.append(g_final)
    return pl.pallas_call(
        functools.partial(_ffn_cast_body, final_norm=g_final is not None),
        grid=(nj,),
        in_specs=in_specs,
        out_specs=[
            pl.BlockSpec((t, d), lambda j: (0, 0)),
            pl.BlockSpec((d, tf), lambda j: (0, j)),
            pl.BlockSpec((d, tf), lambda j: (0, j)),
            pl.BlockSpec((tf, d), lambda j: (j, 0)),
        ],
        out_shape=[jax.ShapeDtypeStruct((t, d), F32), jax.ShapeDtypeStruct((d, d_ff), BF16),
                   jax.ShapeDtypeStruct((d, d_ff), BF16), jax.ShapeDtypeStruct((d_ff, d), BF16)],
        scratch_shapes=[pltpu.VMEM((t, d), BF16), pltpu.VMEM((t, d), F32)],
        compiler_params=_cparams(1, 40),
        name="ffn_cast",
    )(*args)


def _mem_kv_body(m_ref, g_ref, w_ref, kv_ref):
    h = _rms(m_ref[...], g_ref[...]).astype(BF16)
    kv_ref[...] = _dot(h, w_ref[...].astype(BF16))


def _mem_kv(mem, g, w_kv):
    n, d = mem.shape
    depth, _, dkv = w_kv.shape
    return pl.pallas_call(
        _mem_kv_body,
        grid=(depth,),
        in_specs=[
            pl.BlockSpec((n, d), lambda l: (0, 0)),
            pl.BlockSpec((None, 1, d), lambda l: (l, 0, 0)),
            pl.BlockSpec((None, d, dkv), lambda l: (l, 0, 0)),
        ],
        out_specs=pl.BlockSpec((None, n, dkv), lambda l: (l, 0, 0)),
        out_shape=jax.ShapeDtypeStruct((depth, n, dkv), F32),
        compiler_params=_cparams(1, 40),
        name="mem_kv",
    )(mem, g, w_kv)


def _mem_attn_body(x_ref, g_ref, wq_ref, wo_ref, k_ref, v_ref, o_ref):
    x = x_ref[...]
    h = _rms(x, g_ref[...]).astype(BF16)
    hd = k_ref.shape[-1] // MEM_H
    q = (_dot(h, wq_ref[...].astype(BF16)) * (hd ** -0.5)).astype(BF16)
    heads = []
    for hh in range(MEM_H):
        cols = slice(hh * hd, (hh + 1) * hd)
        s = lax.dot_general(q[:, cols], k_ref[:, cols].astype(BF16), _NT, preferred_element_type=F32)
        p = jnp.exp(s - jnp.max(s, axis=-1, keepdims=True))
        o = _dot(p.astype(BF16), v_ref[:, cols].astype(BF16))
        heads.append((o / jnp.sum(p, axis=-1, keepdims=True)).astype(BF16))
    o_ref[...] = x + _dot(jnp.concatenate(heads, axis=-1), wo_ref[...].astype(BF16))


def _mem_attn(x, g, w_q, w_o, k, v, layer, *, tm, kv_layer):
    b, l, d = x.shape
    n_mem, dk = k.shape[-2:]
    kv_spec = pl.BlockSpec((None, None, n_mem, dk), lambda bi, i: (kv_layer, bi, 0, 0))
    return pl.pallas_call(
        _mem_attn_body,
        grid=(b, l // tm),
        in_specs=[
            pl.BlockSpec((None, tm, d), lambda bi, i: (bi, i, 0)),
            pl.BlockSpec((None, 1, d), lambda bi, i: (layer, 0, 0)),
            pl.BlockSpec((None, d, dk), lambda bi, i: (layer, 0, 0)),
            pl.BlockSpec((None, dk, d), lambda bi, i: (layer, 0, 0)),
            kv_spec,
            kv_spec,
        ],
        out_specs=pl.BlockSpec((None, tm, d), lambda bi, i: (bi, i, 0)),
        out_shape=jax.ShapeDtypeStruct((b, l, d), F32),
        compiler_params=_cparams(2, 40),
        name="mem_attn",
    )(x, g, w_q, w_o, k, v)


def _mem_attn_rows_body(x_ref, g_ref, wq_ref, wo_ref, k_ref, v_ref, o_ref, q_scr, a_scr, *, ls):
    b = pl.program_id(0)
    hd = k_ref.shape[-1]

    @pl.when(b == 0)
    def _():
        h = _rms(x_ref[...], g_ref[...]).astype(BF16)
        q_scr[...] = _dot(h, wq_ref[...].astype(BF16)) * (hd ** -0.5)

    rows = pl.ds(pl.multiple_of(b * ls, ls), ls)
    q_seq = q_scr[rows, :]
    q_rows = jnp.concatenate([q_seq[:, hh * hd:(hh + 1) * hd] for hh in range(MEM_H)], axis=0).astype(BF16)
    s = lax.dot_general(q_rows, k_ref[...].astype(BF16), _NT, preferred_element_type=F32)
    row_head = lax.broadcasted_iota(jnp.int32, s.shape, 0) // ls
    col_head = lax.broadcasted_iota(jnp.int32, s.shape, 1) % MEM_H
    s = jnp.where(row_head == col_head, s, -1e30)
    p = jnp.exp(s - jnp.max(s, axis=-1, keepdims=True))
    o = _dot(p.astype(BF16), v_ref[...].astype(BF16)) / jnp.sum(p, axis=-1, keepdims=True)
    a_scr[rows, :] = jnp.concatenate([o[hh * ls:(hh + 1) * ls, :] for hh in range(MEM_H)], axis=1)

    @pl.when(b == pl.num_programs(0) - 1)
    def _():
        o_ref[...] = x_ref[...] + _dot(a_scr[...].astype(BF16), wo_ref[...].astype(BF16))


def _mem_attn_rows(x, g, w_q, w_o, k, v, layer, *, ls):
    t, d = x.shape
    bs = t // ls
    rows_kv, hd = k.shape[-2:]
    dk = MEM_H * hd
    kv_spec = pl.BlockSpec((None, None, rows_kv, hd), lambda b: (layer, b, 0, 0))
    return pl.pallas_call(
        functools.partial(_mem_attn_rows_body, ls=ls),
        grid=(bs,),
        in_specs=[
            pl.BlockSpec((t, d), lambda b: (0, 0)),
            pl.BlockSpec((None, 1, d), lambda b: (layer, 0, 0)),
            pl.BlockSpec((None, d, dk), lambda b: (layer, 0, 0)),
            pl.BlockSpec((None, dk, d), lambda b: (layer, 0, 0)),
            kv_spec,
            kv_spec,
        ],
        out_specs=pl.BlockSpec((t, d), lambda b: (0, 0)),
        out_shape=jax.ShapeDtypeStruct((t, d), F32),
        scratch_shapes=[pltpu.VMEM((t, dk), F32), pltpu.VMEM((t, dk), F32)],
        compiler_params=_cparams(1, 40),
        name="mem_attn_rows",
    )(x, g, w_q, w_o, k, v)


def _chunk_mlp_body(x_ref, g_ref, win_ref, bin_ref, lng_ref, lnb_ref, ws_ref, bs_ref, wout_ref, o_ref, v_ref,
                    v_scr, vb_scr):
    x = x_ref[...]
    tm = x.shape[0]
    d_a = wout_ref.shape[0]
    grp = d_a // H_A
    slab = 2 * grp
    n_slabs = d_a // slab
    h = _rms(x, g_ref[...]).astype(BF16)

    def gelu_slab(c0):
        return jax.nn.gelu(_dot(h, win_ref[:, c0:c0 + slab]) + bin_ref[:, c0:c0 + slab])

    total = jnp.zeros((tm, 1), F32)
    for c in range(n_slabs):
        zc = gelu_slab(d_a + c * slab)
        v_scr[:, c * slab:(c + 1) * slab] = zc
        total = total + jnp.sum(zc, axis=-1, keepdims=True)
    mu = total * (1.0 / d_a)
    sq = jnp.zeros((tm, 1), F32)
    for c in range(n_slabs):
        xc = v_scr[:, c * slab:(c + 1) * slab] - mu
        sq = sq + jnp.sum(xc * xc, axis=-1, keepdims=True)
    rstd = lax.rsqrt(sq * (1.0 / d_a) + EPS)
    for c in range(n_slabs):
        cols = slice(c * slab, (c + 1) * slab)
        vn = (v_scr[:, cols] - mu) * rstd * lng_ref[:, cols] + lnb_ref[:, cols]
        v_scr[:, cols] = vn
        vb_scr[:, cols] = vn.astype(BF16)
    v_rows = v_ref.shape[0]
    v_ref[...] = v_scr[tm - v_rows:, :]

    row = lax.broadcasted_iota(jnp.int32, (CHUNK, CHUNK), 0)
    col = lax.broadcasted_iota(jnp.int32, (CHUNK, CHUNK), 1)
    causal = col <= row
    y = jnp.zeros(x.shape, F32)
    for c in range(n_slabs):
        uc = gelu_slab(c * slab)
        mixed = []
        for hh in (2 * c, 2 * c + 1):
            w_causal = jnp.where(causal, ws_ref[hh], 0.0).astype(BF16)
            cols = slice(hh * grp, (hh + 1) * grp)
            mixed.append(jnp.concatenate(
                [_dot(w_causal, vb_scr[r * CHUNK:(r + 1) * CHUNK, cols]) + bs_ref[hh] for r in range(tm // CHUNK)],
                axis=0))
        um = (uc * jnp.concatenate(mixed, axis=1)).astype(BF16)
        y = y + _dot(um, wout_ref[c * slab:(c + 1) * slab, :])
    o_ref[...] = x + y


def _chunk_mlp(x, g, w_in, b_in, ln_g, ln_b, w_s, b_s, w_out, layer, idx, *, tm, v_rows):
    b, l, d = x.shape
    d_a = w_out.shape[1]
    return pl.pallas_call(
        _chunk_mlp_body,
        grid=(b, l // tm),
        in_specs=[
            pl.BlockSpec((None, tm, d), lambda bi, i: (bi, i, 0)),
            pl.BlockSpec((None, 1, d), lambda bi, i: (layer, 0, 0)),
            pl.BlockSpec((None, d, 2 * d_a), lambda bi, i: (idx, 0, 0), pipeline_mode=pl.Buffered(1)),
            pl.BlockSpec((None, 1, 2 * d_a), lambda bi, i: (idx, 0, 0)),
            pl.BlockSpec((None, 1, d_a), lambda bi, i: (idx, 0, 0)),
            pl.BlockSpec((None, 1, d_a), lambda bi, i: (idx, 0, 0)),
            pl.BlockSpec((H_A, CHUNK, CHUNK), lambda bi, i: (0, 0, 0)),
            pl.BlockSpec((H_A, CHUNK, 1), lambda bi, i: (0, 0, 0)),
            pl.BlockSpec((None, d_a, d), lambda bi, i: (idx, 0, 0), pipeline_mode=pl.Buffered(1)),
        ],
        out_specs=[
            pl.BlockSpec((None, tm, d), lambda bi, i: (bi, i, 0)),
            pl.BlockSpec((None, v_rows, d_a), lambda bi, i: (bi, 0, 0)),
        ],
        out_shape=[jax.ShapeDtypeStruct((b, l, d), F32), jax.ShapeDtypeStruct((b, v_rows, d_a), F32)],
        scratch_shapes=[pltpu.VMEM((tm, d_a), F32), pltpu.VMEM((tm, d_a), BF16)],
        compiler_params=_cparams(2, 60),
        name="chunk_mlp",
    )(x, g, w_in, b_in, ln_g, ln_b, w_s, b_s, w_out)


def _conv_body(x_ref, hist_ref, g_ref, w1_ref, b1_ref, wdw_ref, bdw_ref, lng_ref, lnb_ref, w2_ref, b2_ref,
               o_ref, st_ref, full_scr, win_scr, *, stride):
    i = pl.program_id(1)
    tm, d = x_ref.shape
    nh = (CONV_W - 1) * stride
    hp = full_scr.shape[0] - tm
    h0 = hp - nh

    @pl.when(i == 0)
    def _():
        full_scr[h0:hp, :] = hist_ref[...]

    x = x_ref[...]
    h = _rms(x, g_ref[...]).astype(BF16)
    ag = _dot(h, w1_ref[...]) + b1_ref[...]
    full_scr[hp:, :] = ag[:, :d] * jax.nn.sigmoid(ag[:, d:])
    y = jnp.zeros((tm, d), F32) + bdw_ref[...]
    starts = [h0 + w * stride for w in range(CONV_W)]
    n_windows = 0
    for phase in range(SUBLANES):
        taps = [w for w in range(CONV_W) if starts[w] % SUBLANES == phase]
        if not taps:
            continue
        base = starts[taps[0]]
        rows = starts[taps[-1]] + tm - base
        if phase == 0:
            window, w0 = full_scr, base
        else:
            window, w0 = win_scr.at[n_windows % 2], 0
            window[0:rows, :] = full_scr[base:base + rows, :]
            n_windows += 1
        for w in taps:
            off = w0 + starts[w] - base
            y = y + window[off:off + tm, :] * wdw_ref[w:w + 1, :]
    y = _layernorm(y, lng_ref[...], lnb_ref[...])
    y = (y * jax.nn.sigmoid(y)).astype(BF16)
    o_ref[...] = x + _dot(y, w2_ref[...]) + b2_ref[...]
    new_hist = full_scr[tm + h0:tm + hp, :]
    st_ref[...] = new_hist
    full_scr[h0:hp, :] = new_hist


def _conv_module(x, hist, g, w1, b1, wdw, bdw, ln_g, ln_b, w2, b2, layer, idx, *, tm, stride):
    grp, r, d = x.shape
    nh = (CONV_W - 1) * stride
    hp = -(-nh // SUBLANES) * SUBLANES
    vec = lambda n: pl.BlockSpec((None, 1, n), lambda gi, i: (idx, 0, 0))
    return pl.pallas_call(
        functools.partial(_conv_body, stride=stride),
        grid=(grp, r // tm),
        in_specs=[
            pl.BlockSpec((None, tm, d), lambda gi, i: (gi, i, 0)),
            pl.BlockSpec((None, nh, d), lambda gi, i: (gi, 0, 0)),
            pl.BlockSpec((None, 1, d), lambda gi, i: (layer, 0, 0)),
            pl.BlockSpec((None, d, 2 * d), lambda gi, i: (idx, 0, 0)),
            vec(2 * d),
            pl.BlockSpec((None, CONV_W, d), lambda gi, i: (idx, 0, 0)),
            vec(d), vec(d), vec(d),
            pl.BlockSpec((None, d, d), lambda gi, i: (idx, 0, 0)),
            vec(d),
        ],
        out_specs=[
            pl.BlockSpec((None, tm, d), lambda gi, i: (gi, i, 0)),
            pl.BlockSpec((None, nh, d), lambda gi, i: (gi, 0, 0)),
        ],
        out_shape=[jax.ShapeDtypeStruct((grp, r, d), F32), jax.ShapeDtypeStruct((grp, nh, d), F32)],
        scratch_shapes=[pltpu.VMEM((hp + tm, d), F32),
                        pltpu.VMEM((2, tm + SUBLANES * ((CONV_W - 1) // SUBLANES), d), F32)],
        compiler_params=_cparams(2, 48),
        name="conv_module",
    )(x, hist, g, w1, b1, wdw, bdw, ln_g, ln_b, w2, b2)


def _qkv_body(x_ref, g_ref, w_ref, q_ref, k_ref, v_ref, kb_ref, vb_ref, *, scale, transposed):
    d = x_ref.shape[1]
    h = _rms(x_ref[...], g_ref[...]).astype(BF16)
    qkv = _dot(h, w_ref[...])
    q_ref[...] = (qkv[:, :d] * scale).astype(BF16)
    k = qkv[:, d:2 * d]
    v = qkv[:, 2 * d:]
    k_ref[...] = k.T if transposed else k
    v_ref[...] = v.T if transposed else v
    kb_ref[...] = k.astype(BF16)
    vb_ref[...] = v.astype(BF16)


def _qkv_proj(x, g, w_qkv, layer, idx, *, tm, seq_len=None):
    t, d = x.shape
    row = lambda: pl.BlockSpec((tm, d), lambda i: (i, 0))
    if seq_len is None:
        kv_spec, kv_shape = row, (t, d)
    else:
        per_seq = seq_len // tm
        kv_spec = lambda: pl.BlockSpec((None, d, tm), lambda i: (i // per_seq, 0, i % per_seq))
        kv_shape = (t // seq_len, d, seq_len)
    return pl.pallas_call(
        functools.partial(_qkv_body, scale=(d // H_C) ** -0.5 * LOG2E, transposed=seq_len is not None),
        grid=(t // tm,),
        in_specs=[
            row(),
            pl.BlockSpec((None, 1, d), lambda i: (layer, 0, 0)),
            pl.BlockSpec((None, d, 3 * d), lambda i: (idx, 0, 0)),
        ],
        out_specs=[row(), kv_spec(), kv_spec(), row(), row()],
        out_shape=[jax.ShapeDtypeStruct((t, d), BF16), jax.ShapeDtypeStruct(kv_shape, F32),
                   jax.ShapeDtypeStruct(kv_shape, F32), jax.ShapeDtypeStruct((t, d), BF16),
                   jax.ShapeDtypeStruct((t, d), BF16)],
        compiler_params=_cparams(1, 48),
        name="sb_qkv",
    )(x, g, w_qkv)


def _sb_softplus_cumsum(z2s, u_mat, mask):
    sps = []
    for z2 in z2s:
        sp = jnp.maximum(z2, 0.0) + jnp.log(1.0 + jnp.exp2(-jnp.abs(z2))) * LOG2E
        sps.append(sp if mask is None else jnp.where(mask, sp, 0.0))
    laters = [_dot(sp.astype(BF16), u_mat) for sp in sps]
    return sps, laters


def _sb_weights(z2s, sps, laters, r_prev, mask):
    weights = []
    r = r_prev
    for z2, sp, later in zip(z2s, sps, laters):
        r_rep = jnp.concatenate([r] * (z2.shape[1] // LANES), axis=1)
        a = jnp.exp2(z2 - sp - later - r_rep)
        weights.append(a if mask is None else jnp.where(mask, a, 0.0))
        r = r + jnp.sum(sp, axis=1, keepdims=True)
    return weights, r


def _later_matrix(kb):
    j = lax.broadcasted_iota(jnp.int32, (kb, kb), 0)
    s = lax.broadcasted_iota(jnp.int32, (kb, kb), 1)
    return jnp.where(j > s, 1.0, 0.0).astype(BF16)


def _sb_prompt_body(bias_ref, q_ref, k_ref, v_ref, o_ref, acc_scr, r_scr):
    hp = pl.program_id(1)
    i = pl.program_id(2)
    tq = q_ref.shape[0]
    kb = SB_KEY_BLOCK
    nsub = tq // kb
    half = LANES // 2
    lane = lax.broadcasted_iota(jnp.int32, (1, LANES), 1)
    first = lane < half
    q2 = q_ref[...]
    zq = jnp.zeros_like(q2)
    q_heads = (jnp.where(first, q2, zq), jnp.where(first, zq, q2))
    b_heads = (bias_ref[0, 2 * hp] * LOG2E, bias_ref[0, 2 * hp + 1] * LOG2E)
    u_mat = _later_matrix(kb)
    acc_scr[...] = jnp.zeros_like(acc_scr)
    r_scr[...] = jnp.zeros_like(r_scr)

    def key_blocks(k0s, row0, masked):
        n = tq - row0
        mask = None
        if masked:
            mask = lax.broadcasted_iota(jnp.int32, (n, kb), 1) < lax.broadcasted_iota(jnp.int32, (n, kb), 0)
        kblks = [k_ref[pl.ds(k0, kb), :] for k0 in k0s]
        vblks = [v_ref[pl.ds(k0, kb), :] for k0 in k0s]
        z2s = [[lax.dot_general(q_heads[hh][row0:, :], kblk, _NT, preferred_element_type=F32) + b_heads[hh]
                for kblk in kblks] for hh in range(2)]
        cums = [_sb_softplus_cumsum(z2s[hh], u_mat, mask) for hh in range(2)]
        contrib = jnp.zeros((n, LANES), F32)
        for hh in range(2):
            weights, r_new = _sb_weights(z2s[hh], *cums[hh], r_scr[hh, row0:, :], mask)
            r_scr[hh, row0:, :] = r_new
            v_cat = jnp.concatenate(vblks, axis=0)
            v_head = jnp.where(first, v_cat, 0) if hh == 0 else jnp.where(first, 0, v_cat)
            contrib = contrib + _dot(jnp.concatenate(weights, axis=1).astype(BF16), v_head)
        acc_scr[row0:, :] += contrib

    for c in reversed(range(nsub)):
        key_blocks([pl.multiple_of(i * tq + c * kb, kb)], c * kb, True)

    n_before = i * nsub

    def body(t, carry):
        k_hi = (n_before - 1 - SB_UNROLL * t) * kb
        key_blocks([pl.multiple_of(k_hi - u * kb, kb) for u in range(SB_UNROLL)], 0, False)
        return carry

    lax.fori_loop(0, n_before // SB_UNROLL, body, 0)
    o_ref[...] = acc_scr[...].astype(o_ref.dtype)


def _sb_prompt(q, k, v, b_sb, *, tq):
    b, l, d = q.shape
    n_pairs = d // LANES
    assert l % tq == 0 and tq % (SB_KEY_BLOCK * SB_UNROLL) == 0
    return pl.pallas_call(
        _sb_prompt_body,
        grid=(b, n_pairs, l // tq),
        in_specs=[
            pl.BlockSpec(memory_space=pltpu.SMEM),
            pl.BlockSpec((None, tq, LANES), lambda bi, p, i: (bi, i, p)),
            pl.BlockSpec((None, l, LANES), lambda bi, p, i: (bi, 0, p)),
            pl.BlockSpec((None, l, LANES), lambda bi, p, i: (bi, 0, p)),
        ],
        out_specs=pl.BlockSpec((None, tq, LANES), lambda bi, p, i: (bi, i, p)),
        out_shape=jax.ShapeDtypeStruct((b, l, d), BF16),
        scratch_shapes=[pltpu.VMEM((tq, LANES), F32), pltpu.VMEM((2, tq, LANES), F32)],
        compiler_params=_cparams(3, 48),
        name="sb_prompt",
    )(b_sb, q, k, v)


def _sb_sample_body(pt_ref, q_ref, kn_ref, vn_ref, bias_ref, *rest, pages_per_step):
    del pt_ref
    k_refs = rest[:pages_per_step]
    v_refs = rest[pages_per_step:2 * pages_per_step]
    o_ref, qblk_scr, acc_scr, r_scr = rest[2 * pages_per_step:]
    s = pl.program_id(1)
    ls, d = q_ref.shape
    hd = d // H_C
    kb = LANES
    u_mat = _later_matrix(kb)
    row_head = lax.broadcasted_iota(jnp.int32, (kb, d), 0) // ls
    col_head = lax.broadcasted_iota(jnp.int32, (kb, d), 1) // hd
    own = row_head == col_head
    bias = bias_ref[...] * LOG2E

    @pl.when(s == 0)
    def _():
        q_tiled = jnp.concatenate([q_ref[...]] * H_C, axis=0)
        qblk = jnp.where(own, q_tiled, 0.0).astype(BF16)
        qblk_scr[...] = qblk
        pad = jnp.zeros((kb - ls, d), F32)
        kn = jnp.concatenate([kn_ref[...], pad], axis=0).astype(BF16)
        vn = jnp.concatenate([vn_ref[...], pad], axis=0).astype(BF16)
        z2 = lax.dot_general(qblk, kn, _NT, preferred_element_type=F32) + bias
        qi = lax.broadcasted_iota(jnp.int32, (kb, kb), 0) % ls
        kj = lax.broadcasted_iota(jnp.int32, (kb, kb), 1)
        mask = kj < qi
        sps, laters = _sb_softplus_cumsum([z2], u_mat, mask)
        weights, r_new = _sb_weights([z2], sps, laters, jnp.zeros((kb, LANES), F32), mask)
        r_scr[...] = r_new
        acc_scr[...] = _dot(weights[0].astype(BF16), vn)

    k_cat = jnp.concatenate([k_refs[p][...].astype(BF16) for p in range(pages_per_step)], axis=1)
    v_cat = jnp.concatenate([v_refs[p][...].astype(BF16) for p in range(pages_per_step)], axis=1)
    z2_all = _dot(qblk_scr[...], k_cat)
    z2s = [z2_all[:, p * kb:(p + 1) * kb] + bias for p in range(pages_per_step)]
    sps, laters = _sb_softplus_cumsum(z2s, u_mat, None)
    weights, r_new = _sb_weights(z2s, sps, laters, r_scr[...], None)
    r_scr[...] = r_new
    acc_scr[...] += lax.dot_general(jnp.concatenate(weights, axis=1).astype(BF16), v_cat, _NT,
                                    preferred_element_type=F32)

    @pl.when(s == pl.num_programs(1) - 1)
    def _():
        acc = jnp.where(own, acc_scr[...], 0.0)
        out = acc[0:ls, :]
        for hh in range(1, H_C):
            out = out + acc[hh * ls:(hh + 1) * ls, :]
        o_ref[...] = out.astype(o_ref.dtype)


def _sb_sample(q, k_new, v_new, bias_rows, cache_kt, cache_vt, page_table, *, pages_per_step):
    bs, ls, d = q.shape
    n_pages = page_table.shape[1]
    page = cache_kt.shape[-1]
    n_steps = n_pages // pages_per_step
    kb = LANES

    def page_spec(r):
        return pl.BlockSpec((None, d, page), lambda b, s, pt: (pt[b, n_pages - 1 - (s * pages_per_step + r)], 0, 0))

    seq = lambda: pl.BlockSpec((None, ls, d), lambda b, s, pt: (b, 0, 0))
    grid_spec = pltpu.PrefetchScalarGridSpec(
        num_scalar_prefetch=1,
        grid=(bs, n_steps),
        in_specs=[seq(), seq(), seq(), pl.BlockSpec((kb, kb), lambda b, s, pt: (0, 0))]
        + [page_spec(r) for r in range(pages_per_step)] * 2,
        out_specs=seq(),
        scratch_shapes=[pltpu.VMEM((kb, d), BF16), pltpu.VMEM((kb, d), F32), pltpu.VMEM((kb, LANES), F32)],
    )
    return pl.pallas_call(
        functools.partial(_sb_sample_body, pages_per_step=pages_per_step),
        grid_spec=grid_spec,
        out_shape=jax.ShapeDtypeStruct((bs, ls, d), BF16),
        compiler_params=_cparams(2, 48),
        name="sb_sample",
    )(page_table, q, k_new, v_new, bias_rows, *([cache_kt] * pages_per_step), *([cache_vt] * pages_per_step))


def _out_proj_body(x_ref, o_ref_in, w_ref, y_ref):
    y_ref[...] = x_ref[...] + _dot(o_ref_in[...], w_ref[...].astype(BF16))


def _out_proj(x, o, w, idx, *, tm):
    t, d = x.shape
    return pl.pallas_call(
        _out_proj_body,
        grid=(t // tm,),
        in_specs=[
            pl.BlockSpec((tm, d), lambda i: (i, 0)),
            pl.BlockSpec((tm, d), lambda i: (i, 0)),
            pl.BlockSpec((None, d, d), lambda i: (idx, 0, 0)),
        ],
        out_specs=pl.BlockSpec((tm, d), lambda i: (i, 0)),
        out_shape=jax.ShapeDtypeStruct((t, d), F32),
        compiler_params=_cparams(1, 40),
        name="sb_out_proj",
    )(x, o, w)


def kernel(x_prompt, x_sample, mem_prompt, cache_conv, cache_sb_k, cache_sb_v, cache_mem_k, cache_mem_v, page_table, g_ffn1, f1_w_gu, f1_w_d, g_mix, a_w_in, a_b_in, a_ln_g, a_ln_b, a_w_s, a_b_s, a_w_out, b_w_pw1, b_b_pw1, b_w_dw, b_b_dw, b_ln_g, b_ln_b, b_w_pw2, b_b_pw2, c_w_qkv, c_b_sb, c_w_o, g_mem_q, g_mem_kv, m_w_q, m_w_kv, m_w_o, g_ffn2, f2_w_gu, f2_w_d, g_final):
    bp, lp, d = x_prompt.shape
    bs, ls, _ = x_sample.shape
    depth = g_ffn1.shape[0]
    n_mem = mem_prompt.shape[1]
    d_a = a_w_out.shape[1]
    tp, ts = bp * lp, bs * ls
    page = cache_sb_k.shape[2]
    seqs_per_chunk = CHUNK // ls

    row3 = lambda a: a[:, None, :]
    g1, gm, gq, gkv, g2 = (row3(a) for a in (g_ffn1, g_mix, g_mem_q, g_mem_kv, g_ffn2))
    gf = g_final[None, :]

    kv_mem = _mem_kv(mem_prompt.reshape(bp * n_mem, d), gkv, m_w_kv)
    dk = kv_mem.shape[-1] // 2
    mk_p = kv_mem[..., :dk].reshape(depth, bp, n_mem, dk)
    mv_p = kv_mem[..., dk:].reshape(depth, bp, n_mem, dk)
    mk_s = cache_mem_k.reshape(depth, bs, n_mem * MEM_H, dk // MEM_H)
    mv_s = cache_mem_v.reshape(depth, bs, n_mem * MEM_H, dk // MEM_H)

    eye = jnp.eye(seqs_per_chunk, dtype=F32)
    a_w_in_b = a_w_in.astype(BF16)
    a_w_out_b = a_w_out.astype(BF16)
    w1_b = b_w_pw1.astype(BF16)
    w2_b = b_w_pw2.astype(BF16)
    wqkv_b = c_w_qkv.astype(BF16)

    yp = x_prompt.reshape(tp, d)
    ys = x_sample.reshape(ts, d)
    chunk_v_p, chunk_v_s, conv_p, conv_s = [], [], [], []
    k_p, v_p, k_s, v_s = [], [], [], []

    for l in range(depth):
        kind, i = l % 3, l // 3
        ys, *w_bf16 = _ffn_cast(ys, g1, f1_w_gu, f1_w_d, l)
        yp = _ffn(yp, g1, *w_bf16, l, tm=1024)
        if kind == 0:
            bias_p = a_b_s[i][:, :, None]
            yp3, vrows_p = _chunk_mlp(yp.reshape(bp, lp, d), gm, a_w_in_b, row3(a_b_in), row3(a_ln_g), row3(a_ln_b),
                                      a_w_s[i], bias_p, a_w_out_b, l, i, tm=512, v_rows=CHUNK)
            w_s_blk = jnp.einsum('ab,hts->hatbs', eye, a_w_s[i][:, :ls, :ls]).reshape(H_A, CHUNK, CHUNK)
            bias_s = jnp.tile(a_b_s[i][:, :ls], (1, seqs_per_chunk))[:, :, None]
            ys3, vrows_s = _chunk_mlp(ys.reshape(1, ts, d), gm, a_w_in_b, row3(a_b_in), row3(a_ln_g), row3(a_ln_b),
                                      w_s_blk, bias_s, a_w_out_b, l, i, tm=ts, v_rows=ts)
            yp, ys = yp3.reshape(tp, d), ys3.reshape(ts, d)
            chunk_v_p.append(vrows_p)
            chunk_v_s.append(vrows_s.reshape(bs, ls, d_a))
        elif kind == 1:
            conv_args = (gm, w1_b, row3(b_b_pw1), b_w_dw, row3(b_b_dw), row3(b_ln_g), row3(b_ln_b), w2_b, row3(b_b_pw2))
            zeros_hist = jnp.zeros((bp, CONV_W - 1, d), F32)
            yp3, hist_p = _conv_module(yp.reshape(bp, lp, d), zeros_hist, *conv_args, l, i, tm=512, stride=1)
            ys_t = ys.reshape(bs, ls, d).transpose(1, 0, 2).reshape(1, ts, d)
            hist_t = cache_conv[i].transpose(1, 0, 2).reshape(1, (CONV_W - 1) * bs, d)
            ys_t, hist_s = _conv_module(ys_t, hist_t, *conv_args, l, i, tm=ts, stride=bs)
            yp = yp3.reshape(tp, d)
            ys = ys_t.reshape(ls, bs, d).transpose(1, 0, 2).reshape(ts, d)
            conv_p.append(hist_p)
            conv_s.append(hist_s.reshape(CONV_W - 1, bs, d).transpose(1, 0, 2))
        else:
            qp, kp_t, vp_t, kpb, vpb = _qkv_proj(yp, gm, wqkv_b, l, i, tm=512, seq_len=lp)
            shp = (bp, lp, d)
            op = _sb_prompt(qp.reshape(shp), kpb.reshape(shp), vpb.reshape(shp), c_b_sb[i][None, :], tq=1024)
            yp = _out_proj(yp, op.reshape(tp, d), c_w_o, i, tm=1024)
            qs, ks_new, vs_new, _, _ = _qkv_proj(ys, gm, wqkv_b, l, i, tm=ts)
            kt = cache_sb_k[i].transpose(0, 2, 3, 1).reshape(-1, d, page)
            vt = cache_sb_v[i].transpose(0, 2, 3, 1).reshape(-1, d, page)
            bias_rows = jnp.broadcast_to(jnp.repeat(c_b_sb[i], ls)[:, None], (H_C * ls, LANES))
            shs = (bs, ls, d)
            os_ = _sb_sample(qs.reshape(shs).astype(F32), ks_new.reshape(shs), vs_new.reshape(shs), bias_rows,
                             kt, vt, page_table, pages_per_step=8)
            ys = _out_proj(ys, os_.reshape(ts, d), c_w_o, i, tm=ts)
            hd = d // H_C
            k_p.append(kp_t.reshape(bp, H_C, hd, lp).transpose(0, 3, 1, 2))
            v_p.append(vp_t.reshape(bp, H_C, hd, lp).transpose(0, 3, 1, 2))
            k_s.append(ks_new.reshape(bs, ls, H_C, hd))
            v_s.append(vs_new.reshape(bs, ls, H_C, hd))
        yp = _mem_attn(yp.reshape(bp, lp, d), gq, m_w_q, m_w_o, mk_p, mv_p, l, tm=1024, kv_layer=l).reshape(tp, d)
        ys = _mem_attn_rows(ys, gq, m_w_q, m_w_o, mk_s, mv_s, l, ls=ls)
        last = l == depth - 1
        ys, *w_bf16 = _ffn_cast(ys, g2, f2_w_gu, f2_w_d, l, g_final=gf if last else None)
        yp = _ffn(yp, g2, *w_bf16, l, tm=1024, g_final=gf if last else None)

    mem_hd = dk // MEM_H
    return (yp.reshape(bp, lp, d), ys.reshape(bs, ls, d),
            jnp.stack(chunk_v_p), jnp.stack(chunk_v_s), jnp.stack(conv_p), jnp.stack(conv_s),
            jnp.stack(k_p), jnp.stack(v_p), jnp.stack(k_s), jnp.stack(v_s),
            mk_p.reshape(depth, bp, n_mem, MEM_H, mem_hd), mv_p.reshape(depth, bp, n_mem, MEM_H, mem_hd))
```

```python
import functools

import jax
import jax.numpy as jnp
from jax import lax
from jax.experimental import pallas as pl
from jax.experimental.pallas import tpu as pltpu

F32 = jnp.float32
BF16 = jnp.bfloat16
EPS = 1e-6

CHUNK = 128
H_A = 8
CONV_W = 31
H_C = 16
MEM_H = 4
SB_KEY_BLOCK = 256
SB_UNROLL = 4
LOG2E = 1.4426950408889634
LANES = 128
SUBLANES = 8
MIB = 1 << 20

TM_FFN = 1024
TM_CHUNK = 512
TM_CONV = 512
TM_QKV = 512
TM_MEM = 1024
TQ_SB = 1024
SB_PAGES_PER_STEP = 16

_NT =(((1,), (1,)), ((), ()))


def _cparams(n_axes, vmem_mib):
    return pltpu.CompilerParams(dimension_semantics=("arbitrary",) * n_axes, vmem_limit_bytes=vmem_mib * MIB)


def _rms(x, g):
    return x * lax.rsqrt(jnp.mean(x * x, axis=-1, keepdims=True) + EPS) * g


def _layernorm(x, g, b):
    mu = jnp.mean(x, axis=-1, keepdims=True)
    xc = x - mu
    var = jnp.mean(xc * xc, axis=-1, keepdims=True)
    return xc * lax.rsqrt(var + EPS) * g + b


def _dot(a, b):
    return jnp.dot(a, b, preferred_element_type=F32)


def _ffn_body(x_ref, g_ref, wg_ref, wu_ref, wd_ref, *rest, final_norm):
    gf_ref = rest[0] if final_norm else None
    o_ref, h_scr, acc_scr = rest[-3:]
    j = pl.program_id(1)

    @pl.when(j == 0)
    def _():
        h_scr[...] = _rms(x_ref[...], g_ref[...]).astype(BF16)
        acc_scr[...] = jnp.zeros_like(acc_scr)

    h = h_scr[...]
    gate = _dot(h, wg_ref[...])
    up = _dot(h, wu_ref[...])
    a = (gate * jax.nn.sigmoid(gate) * up).astype(BF16)
    acc_scr[...] += _dot(a, wd_ref[...])

    @pl.when(j == pl.num_programs(1) - 1)
    def _():
        y = x_ref[...] + 0.5 * acc_scr[...]
        if final_norm:
            y = _rms(y, gf_ref[...])
        o_ref[...] = y


def _ffn(x, g, w_g, w_u, w_d, layer, *, tm, tf=1024, g_final=None):
    t, d = x.shape
    d_ff = w_d.shape[0]
    nj = d_ff // tf
    in_specs = [
        pl.BlockSpec((tm, d), lambda i, j: (i, 0)),
        pl.BlockSpec((None, 1, d), lambda i, j: (layer, 0, 0)),
        pl.BlockSpec((d, tf), lambda i, j: (0, j)),
        pl.BlockSpec((d, tf), lambda i, j: (0, j)),
        pl.BlockSpec((tf, d), lambda i, j: (j, 0)),
    ]
    args = [x, g, w_g, w_u, w_d]
    if g_final is not None:
        in_specs.append(pl.BlockSpec((1, d), lambda i, j: (0, 0)))
        args.append(g_final)
    return pl.pallas_call(
        functools.partial(_ffn_body, final_norm=g_final is not None),
        grid=(t // tm, nj),
        in_specs=in_specs,
        out_specs=pl.BlockSpec((tm, d), lambda i, j: (i, 0)),
        out_shape=jax.ShapeDtypeStruct((t, d), F32),
        scratch_shapes=[pltpu.VMEM((tm, d), BF16), pltpu.VMEM((tm, d), F32)],
        compiler_params=_cparams(2, 56),
        name="ffn",
    )(*args)


def _ffn_cast_body(x_ref, g_ref, wg_ref, wu_ref, wd_ref, *rest, final_norm):
    gf_ref = rest[0] if final_norm else None
    o_ref, wg_out, wu_out, wd_out, h_scr, acc_scr = rest[-6:]
    j = pl.program_id(0)

    @pl.when(j == 0)
    def _():
        h_scr[...] = _rms(x_ref[...], g_ref[...]).astype(BF16)
        acc_scr[...] = jnp.zeros_like(acc_scr)

    wg = wg_ref[...].astype(BF16)
    wu = wu_ref[...].astype(BF16)
    wd = wd_ref[...].astype(BF16)
    wg_out[...] = wg
    wu_out[...] = wu
    wd_out[...] = wd
    h = h_scr[...]
    gate = _dot(h, wg)
    up = _dot(h, wu)
    a = (gate * jax.nn.sigmoid(gate) * up).astype(BF16)
    acc_scr[...] += _dot(a, wd)

    @pl.when(j == pl.num_programs(0) - 1)
    def _():
        y = x_ref[...] + 0.5 * acc_scr[...]
        if final_norm:
            y = _rms(y, gf_ref[...])
        o_ref[...] = y


def _ffn_cast(x, g, w_gu, w_d, layer, *, tf=512, g_final=None):
    t, d = x.shape
    d_ff = w_d.shape[1]
    nj = d_ff // tf
    in_specs = [
        pl.BlockSpec((t, d), lambda j: (0, 0)),
        pl.BlockSpec((None, 1, d), lambda j: (layer, 0, 0)),
        pl.BlockSpec((None, d, tf), lambda j: (layer, 0, j)),
        pl.BlockSpec((None, d, tf), lambda j: (layer, 0, j + nj)),
        pl.BlockSpec((None, tf, d), lambda j: (layer, j, 0)),
    ]
    args = [x, g, w_gu, w_gu, w_d]
    if g_final is not None:
        in_specs.append(pl.BlockSpec((1, d), lambda j: (0, 0)))
        args.append(g_final)
    return pl.pallas_call(
        functools.partial(_ffn_cast_body, final_norm=g_final is not None),
        grid=(nj,),
        in_specs=in_specs,
        out_specs=[
            pl.BlockSpec((t, d), lambda j: (0, 0)),
            pl.BlockSpec((d, tf), lambda j: (0, j)),
            pl.BlockSpec((d, tf), lambda j: (0, j)),
            pl.BlockSpec((tf, d), lambda j: (j, 0)),
        ],
        out_shape=[jax.ShapeDtypeStruct((t, d), F32), jax.ShapeDtypeStruct((d, d_ff), BF16),
                   jax.ShapeDtypeStruct((d, d_ff), BF16), jax.ShapeDtypeStruct((d_ff, d), BF16)],
        scratch_shapes=[pltpu.VMEM((t, d), BF16), pltpu.VMEM((t, d), F32)],
        compiler_params=_cparams(1, 40),
        name="ffn_cast",
    )(*args)


def _mem_kv_body(m_ref, g_ref, w_ref, kv_ref):
    h = _rms(m_ref[...], g_ref[...]).astype(BF16)
    kv_ref[...] = _dot(h, w_ref[...].astype(BF16))


def _mem_kv(mem, g, w_kv):
    n, d = mem.shape
    depth, _, dkv = w_kv.shape
    return pl.pallas_call(
        _mem_kv_body,
        grid=(depth,),
        in_specs=[
            pl.BlockSpec((n, d), lambda l: (0, 0)),
            pl.BlockSpec((None, 1, d), lambda l: (l, 0, 0)),
            pl.BlockSpec((None, d, dkv), lambda l: (l, 0, 0)),
        ],
        out_specs=pl.BlockSpec((None, n, dkv), lambda l: (l, 0, 0)),
        out_shape=jax.ShapeDtypeStruct((depth, n, dkv), F32),
        compiler_params=_cparams(1, 40),
        name="mem_kv",
    )(mem, g, w_kv)


def _mem_attn_body(x_ref, g_ref, wq_ref, wo_ref, k_ref, v_ref, *rest, pre_proj):
    if pre_proj:
        a_ref, wa_ref, o_ref = rest
        x = x_ref[...] + _dot(a_ref[...], wa_ref[...].astype(BF16))
    else:
        (o_ref,) = rest
        x = x_ref[...]
    h = _rms(x, g_ref[...]).astype(BF16)
    hd = k_ref.shape[-1] // MEM_H
    q = (_dot(h, wq_ref[...].astype(BF16)) * (hd ** -0.5)).astype(BF16)
    heads = []
    for hh in range(MEM_H):
        cols = slice(hh * hd, (hh + 1) * hd)
        s = lax.dot_general(q[:, cols], k_ref[:, cols].astype(BF16), _NT, preferred_element_type=F32)
        p = jnp.exp(s - jnp.max(s, axis=-1, keepdims=True))
        o = _dot(p.astype(BF16), v_ref[:, cols].astype(BF16))
        heads.append((o / jnp.sum(p, axis=-1, keepdims=True)).astype(BF16))
    o_ref[...] = x + _dot(jnp.concatenate(heads, axis=-1), wo_ref[...].astype(BF16))


def _mem_attn(x, g, w_q, w_o, k, v, layer, *, tm, kv_layer, attn=None, w_attn=None, attn_idx=0):
    b, l, d = x.shape
    n_mem, dk = k.shape[-2:]
    kv_spec = pl.BlockSpec((None, None, n_mem, dk), lambda bi, i: (kv_layer, bi, 0, 0))
    rows = pl.BlockSpec((None, tm, d), lambda bi, i: (bi, i, 0))
    in_specs = [
        rows,
        pl.BlockSpec((None, 1, d), lambda bi, i: (layer, 0, 0)),
        pl.BlockSpec((None, d, dk), lambda bi, i: (layer, 0, 0)),
        pl.BlockSpec((None, dk, d), lambda bi, i: (layer, 0, 0)),
        kv_spec,
        kv_spec,
    ]
    args = [x, g, w_q, w_o, k, v]
    if attn is not None:
        in_specs += [rows, pl.BlockSpec((None, d, d), lambda bi, i: (attn_idx, 0, 0))]
        args += [attn, w_attn]
    return pl.pallas_call(
        functools.partial(_mem_attn_body, pre_proj=attn is not None),
        grid=(b, l // tm),
        in_specs=in_specs,
        out_specs=rows,
        out_shape=jax.ShapeDtypeStruct((b, l, d), F32),
        compiler_params=_cparams(2, 48),
        name="mem_attn",
    )(*args)


def _mem_attn_rows_body(x_ref, g_ref, wq_ref, wo_ref, k_ref, v_ref, o_ref, q_scr, a_scr, *, ls):
    b = pl.program_id(0)
    hd = k_ref.shape[-1]

    @pl.when(b == 0)
    def _():
        h = _rms(x_ref[...], g_ref[...]).astype(BF16)
        q_scr[...] = _dot(h, wq_ref[...].astype(BF16)) * (hd ** -0.5)

    rows = pl.ds(pl.multiple_of(b * ls, ls), ls)
    q_seq = q_scr[rows, :]
    q_rows = jnp.concatenate([q_seq[:, hh * hd:(hh + 1) * hd] for hh in range(MEM_H)], axis=0).astype(BF16)
    s = lax.dot_general(q_rows, k_ref[...].astype(BF16), _NT, preferred_element_type=F32)
    row_head = lax.broadcasted_iota(jnp.int32, s.shape, 0) // ls
    col_head = lax.broadcasted_iota(jnp.int32, s.shape, 1) % MEM_H
    s = jnp.where(row_head == col_head, s, -1e30)
    p = jnp.exp(s - jnp.max(s, axis=-1, keepdims=True))
    o = _dot(p.astype(BF16), v_ref[...].astype(BF16)) / jnp.sum(p, axis=-1, keepdims=True)
    a_scr[rows, :] = jnp.concatenate([o[hh * ls:(hh + 1) * ls, :] for hh in range(MEM_H)], axis=1)

    @pl.when(b == pl.num_programs(0) - 1)
    def _():
        o_ref[...] = x_ref[...] + _dot(a_scr[...].astype(BF16), wo_ref[...].astype(BF16))


def _mem_attn_rows(x, g, w_q, w_o, k, v, layer, *, ls):
    t, d = x.shape
    bs = t // ls
    rows_kv, hd = k.shape[-2:]
    dk = MEM_H * hd
    kv_spec = pl.BlockSpec((None, None, rows_kv, hd), lambda b: (layer, b, 0, 0))
    return pl.pallas_call(
        functools.partial(_mem_attn_rows_body, ls=ls),
        grid=(bs,),
        in_specs=[
            pl.BlockSpec((t, d), lambda b: (0, 0)),
            pl.BlockSpec((None, 1, d), lambda b: (layer, 0, 0)),
            pl.BlockSpec((None, d, dk), lambda b: (layer, 0, 0)),
            pl.BlockSpec((None, dk, d), lambda b: (layer, 0, 0)),
            kv_spec,
            kv_spec,
        ],
        out_specs=pl.BlockSpec((t, d), lambda b: (0, 0)),
        out_shape=jax.ShapeDtypeStruct((t, d), F32),
        scratch_shapes=[pltpu.VMEM((t, dk), F32), pltpu.VMEM((t, dk), F32)],
        compiler_params=_cparams(1, 40),
        name="mem_attn_rows",
    )(x, g, w_q, w_o, k, v)


def _chunk_mlp_body(x_ref, g_ref, win_ref, bin_ref, lng_ref, lnb_ref, ws_ref, bs_ref, wout_ref, o_ref, v_ref,
                    v_scr, vb_scr):
    x = x_ref[...]
    tm = x.shape[0]
    d_a = wout_ref.shape[0]
    grp = d_a // H_A
    slab = 2 * grp
    n_slabs = d_a // slab
    h = _rms(x, g_ref[...]).astype(BF16)

    def gelu_slab(c0):
        return jax.nn.gelu(_dot(h, win_ref[:, c0:c0 + slab]) + bin_ref[:, c0:c0 + slab])

    total = jnp.zeros((tm, 1), F32)
    for c in range(n_slabs):
        zc = gelu_slab(d_a + c * slab)
        v_scr[:, c * slab:(c + 1) * slab] = zc
        total = total + jnp.sum(zc, axis=-1, keepdims=True)
    mu = total * (1.0 / d_a)
    sq = jnp.zeros((tm, 1), F32)
    for c in range(n_slabs):
        xc = v_scr[:, c * slab:(c + 1) * slab] - mu
        sq = sq + jnp.sum(xc * xc, axis=-1, keepdims=True)
    rstd = lax.rsqrt(sq * (1.0 / d_a) + EPS)
    for c in range(n_slabs):
        cols = slice(c * slab, (c + 1) * slab)
        vn = (v_scr[:, cols] - mu) * rstd * lng_ref[:, cols] + lnb_ref[:, cols]
        v_scr[:, cols] = vn
        vb_scr[:, cols] = vn.astype(BF16)
    v_rows = v_ref.shape[0]
    v_ref[...] = v_scr[tm - v_rows:, :]

    row = lax.broadcasted_iota(jnp.int32, (CHUNK, CHUNK), 0)
    col = lax.broadcasted_iota(jnp.int32, (CHUNK, CHUNK), 1)
    causal = col <= row
    y = jnp.zeros(x.shape, F32)
    for c in range(n_slabs):
        uc = gelu_slab(c * slab)
        mixed = []
        for hh in (2 * c, 2 * c + 1):
            w_causal = jnp.where(causal, ws_ref[hh], 0.0).astype(BF16)
            cols = slice(hh * grp, (hh + 1) * grp)
            mixed.append(jnp.concatenate(
                [_dot(w_causal, vb_scr[r * CHUNK:(r + 1) * CHUNK, cols]) + bs_ref[hh] for r in range(tm // CHUNK)],
                axis=0))
        um = (uc * jnp.concatenate(mixed, axis=1)).astype(BF16)
        y = y + _dot(um, wout_ref[c * slab:(c + 1) * slab, :])
    o_ref[...] = x + y


def _chunk_mlp(x, g, w_in, b_in, ln_g, ln_b, w_s, b_s, w_out, layer, idx, *, tm, v_rows):
    b, l, d = x.shape
    d_a = w_out.shape[1]
    return pl.pallas_call(
        _chunk_mlp_body,
        grid=(b, l // tm),
        in_specs=[
            pl.BlockSpec((None, tm, d), lambda bi, i: (bi, i, 0)),
            pl.BlockSpec((None, 1, d), lambda bi, i: (layer, 0, 0)),
            pl.BlockSpec((None, d, 2 * d_a), lambda bi, i: (idx, 0, 0), pipeline_mode=pl.Buffered(1)),
            pl.BlockSpec((None, 1, 2 * d_a), lambda bi, i: (idx, 0, 0)),
            pl.BlockSpec((None, 1, d_a), lambda bi, i: (idx, 0, 0)),
            pl.BlockSpec((None, 1, d_a), lambda bi, i: (idx, 0, 0)),
            pl.BlockSpec((H_A, CHUNK, CHUNK), lambda bi, i: (0, 0, 0)),
            pl.BlockSpec((H_A, CHUNK, 1), lambda bi, i: (0, 0, 0)),
            pl.BlockSpec((None, d_a, d), lambda bi, i: (idx, 0, 0), pipeline_mode=pl.Buffered(1)),
        ],
        out_specs=[
            pl.BlockSpec((None, tm, d), lambda bi, i: (bi, i, 0)),
            pl.BlockSpec((None, v_rows, d_a), lambda bi, i: (bi, 0, 0)),
        ],
        out_shape=[jax.ShapeDtypeStruct((b, l, d), F32), jax.ShapeDtypeStruct((b, v_rows, d_a), F32)],
        scratch_shapes=[pltpu.VMEM((tm, d_a), F32), pltpu.VMEM((tm, d_a), BF16)],
        compiler_params=_cparams(2, 60),
        name="chunk_mlp",
    )(x, g, w_in, b_in, ln_g, ln_b, w_s, b_s, w_out)


def _conv_body(x_ref, hist_ref, g_ref, w1_ref, b1_ref, wdw_ref, bdw_ref, lng_ref, lnb_ref, w2_ref, b2_ref,
               o_ref, st_ref, full_scr, win_scr, *, stride):
    i = pl.program_id(1)
    tm, d = x_ref.shape
    nh = (CONV_W - 1) * stride
    hp = full_scr.shape[0] - tm
    h0 = hp - nh

    @pl.when(i == 0)
    def _():
        full_scr[h0:hp, :] = hist_ref[...]

    x = x_ref[...]
    h = _rms(x, g_ref[...]).astype(BF16)
    ag = _dot(h, w1_ref[...]) + b1_ref[...]
    full_scr[hp:, :] = ag[:, :d] * jax.nn.sigmoid(ag[:, d:])
    y = jnp.zeros((tm, d), F32) + bdw_ref[...]
    starts = [h0 + w * stride for w in range(CONV_W)]
    n_windows = 0
    for phase in range(SUBLANES):
        taps = [w for w in range(CONV_W) if starts[w] % SUBLANES == phase]
        if not taps:
            continue
        base = starts[taps[0]]
        rows = starts[taps[-1]] + tm - base
        if phase == 0:
            window, w0 = full_scr, base
        else:
            window, w0 = win_scr.at[n_windows % 2], 0
            window[0:rows, :] = full_scr[base:base + rows, :]
            n_windows += 1
        for w in taps:
            off = w0 + starts[w] - base
            y = y + window[off:off + tm, :] * wdw_ref[w:w + 1, :]
    y = _layernorm(y, lng_ref[...], lnb_ref[...])
    y = (y * jax.nn.sigmoid(y)).astype(BF16)
    o_ref[...] = x + _dot(y, w2_ref[...]) + b2_ref[...]
    new_hist = full_scr[tm + h0:tm + hp, :]
    st_ref[...] = new_hist
    full_scr[h0:hp, :] = new_hist


def _conv_module(x, hist, g, w1, b1, wdw, bdw, ln_g, ln_b, w2, b2, layer, idx, *, tm, stride):
    grp, r, d = x.shape
    nh = (CONV_W - 1) * stride
    hp = -(-nh // SUBLANES) * SUBLANES
    vec = lambda n: pl.BlockSpec((None, 1, n), lambda gi, i: (idx, 0, 0))
    return pl.pallas_call(
        functools.partial(_conv_body, stride=stride),
        grid=(grp, r // tm),
        in_specs=[
            pl.BlockSpec((None, tm, d), lambda gi, i: (gi, i, 0)),
            pl.BlockSpec((None, nh, d), lambda gi, i: (gi, 0, 0)),
            pl.BlockSpec((None, 1, d), lambda gi, i: (layer, 0, 0)),
            pl.BlockSpec((None, d, 2 * d), lambda gi, i: (idx, 0, 0)),
            vec(2 * d),
            pl.BlockSpec((None, CONV_W, d), lambda gi, i: (idx, 0, 0)),
            vec(d), vec(d), vec(d),
            pl.BlockSpec((None, d, d), lambda gi, i: (idx, 0, 0)),
            vec(d),
        ],
        out_specs=[
            pl.BlockSpec((None, tm, d), lambda gi, i: (gi, i, 0)),
            pl.BlockSpec((None, nh, d), lambda gi, i: (gi, 0, 0)),
        ],
        out_shape=[jax.ShapeDtypeStruct((grp, r, d), F32), jax.ShapeDtypeStruct((grp, nh, d), F32)],
        scratch_shapes=[pltpu.VMEM((hp + tm, d), F32),
                        pltpu.VMEM((2, tm + SUBLANES * ((CONV_W - 1) // SUBLANES), d), F32)],
        compiler_params=_cparams(2, 48),
        name="conv_module",
    )(x, hist, g, w1, b1, wdw, bdw, ln_g, ln_b, w2, b2)


def _qkv_body(x_ref, g_ref, w_ref, q_ref, k_ref, v_ref, kb_ref, vb_ref, *, scale, transposed):
    d = x_ref.shape[1]
    h = _rms(x_ref[...], g_ref[...]).astype(BF16)
    qkv = _dot(h, w_ref[...])
    q_ref[...] = (qkv[:, :d] * scale).astype(BF16)
    k = qkv[:, d:2 * d]
    v = qkv[:, 2 * d:]
    k_ref[...] = k.T if transposed else k
    v_ref[...] = v.T if transposed else v
    kb_ref[...] = k.astype(BF16)
    vb_ref[...] = v.astype(BF16)


def _qkv_proj(x, g, w_qkv, layer, idx, *, tm, seq_len=None):
    t, d = x.shape
    row = lambda: pl.BlockSpec((tm, d), lambda i: (i, 0))
    if seq_len is None:
        kv_spec, kv_shape = row, (t, d)
    else:
        per_seq = seq_len // tm
        kv_spec = lambda: pl.BlockSpec((None, d, tm), lambda i: (i // per_seq, 0, i % per_seq))
        kv_shape = (t // seq_len, d, seq_len)
    return pl.pallas_call(
        functools.partial(_qkv_body, scale=(d // H_C) ** -0.5 * LOG2E, transposed=seq_len is not None),
        grid=(t // tm,),
        in_specs=[
            row(),
            pl.BlockSpec((None, 1, d), lambda i: (layer, 0, 0)),
            pl.BlockSpec((None, d, 3 * d), lambda i: (idx, 0, 0)),
        ],
        out_specs=[row(), kv_spec(), kv_spec(), row(), row()],
        out_shape=[jax.ShapeDtypeStruct((t, d), BF16), jax.ShapeDtypeStruct(kv_shape, F32),
                   jax.ShapeDtypeStruct(kv_shape, F32), jax.ShapeDtypeStruct((t, d), BF16),
                   jax.ShapeDtypeStruct((t, d), BF16)],
        compiler_params=_cparams(1, 48),
        name="sb_qkv",
    )(x, g, w_qkv)


def _sb_softplus_cumsum(z2s, u_mat, mask):
    sps = []
    for z2 in z2s:
        sp = jnp.maximum(z2, 0.0) + jnp.log(1.0 + jnp.exp2(-jnp.abs(z2))) * LOG2E
        sps.append(sp if mask is None else jnp.where(mask, sp, 0.0))
    laters = [_dot(sp.astype(BF16), u_mat) for sp in sps]
    return sps, laters


def _sb_weights(z2s, sps, laters, r_prev, mask):
    weights = []
    r = r_prev
    for z2, sp, later in zip(z2s, sps, laters):
        r_rep = jnp.concatenate([r] * (z2.shape[1] // LANES), axis=1)
        a = jnp.exp2(z2 - sp - later - r_rep)
        weights.append(a if mask is None else jnp.where(mask, a, 0.0))
        r = r + jnp.sum(sp, axis=1, keepdims=True)
    return weights, r


def _later_matrix(kb):
    j = lax.broadcasted_iota(jnp.int32, (kb, kb), 0)
    s = lax.broadcasted_iota(jnp.int32, (kb, kb), 1)
    return jnp.where(j > s, 1.0, 0.0).astype(BF16)


def _sb_prompt_body(bias_ref, q_ref, k_ref, v_ref, o_ref, acc_scr, r_scr):
    hp = pl.program_id(1)
    i = pl.program_id(2)
    tq = q_ref.shape[0]
    kb = SB_KEY_BLOCK
    nsub = tq // kb
    half = LANES // 2
    lane = lax.broadcasted_iota(jnp.int32, (1, LANES), 1)
    first = lane < half
    q2 = q_ref[...]
    zq = jnp.zeros_like(q2)
    q_heads = (jnp.where(first, q2, zq), jnp.where(first, zq, q2))
    b_heads = (bias_ref[0, 2 * hp] * LOG2E, bias_ref[0, 2 * hp + 1] * LOG2E)
    u_mat = _later_matrix(kb)
    acc_scr[...] = jnp.zeros_like(acc_scr)
    r_scr[...] = jnp.zeros_like(r_scr)

    def key_blocks(k0s, row0, masked):
        n = tq - row0
        mask = None
        if masked:
            mask = lax.broadcasted_iota(jnp.int32, (n, kb), 1) < lax.broadcasted_iota(jnp.int32, (n, kb), 0)
        kblks = [k_ref[pl.ds(k0, kb), :] for k0 in k0s]
        vblks = [v_ref[pl.ds(k0, kb), :] for k0 in k0s]
        z2s = [[lax.dot_general(q_heads[hh][row0:, :], kblk, _NT, preferred_element_type=F32) + b_heads[hh]
                for kblk in kblks] for hh in range(2)]
        cums = [_sb_softplus_cumsum(z2s[hh], u_mat, mask) for hh in range(2)]
        contrib = jnp.zeros((n, LANES), F32)
        for hh in range(2):
            weights, r_new = _sb_weights(z2s[hh], *cums[hh], r_scr[hh, row0:, :], mask)
            r_scr[hh, row0:, :] = r_new
            v_cat = jnp.concatenate(vblks, axis=0)
            v_head = jnp.where(first, v_cat, 0) if hh == 0 else jnp.where(first, 0, v_cat)
            contrib = contrib + _dot(jnp.concatenate(weights, axis=1).astype(BF16), v_head)
        acc_scr[row0:, :] += contrib

    for c in reversed(range(nsub)):
        key_blocks([pl.multiple_of(i * tq + c * kb, kb)], c * kb, True)

    n_before = i * nsub

    def body(t, carry):
        k_hi = (n_before - 1 - SB_UNROLL * t) * kb
        key_blocks([pl.multiple_of(k_hi - u * kb, kb) for u in range(SB_UNROLL)], 0, False)
        return carry

    lax.fori_loop(0, n_before // SB_UNROLL, body, 0)
    o_ref[...] = acc_scr[...].astype(o_ref.dtype)


def _sb_prompt(q, k, v, b_sb, *, tq):
    b, l, d = q.shape
    n_pairs = d // LANES
    assert l % tq == 0 and tq % (SB_KEY_BLOCK * SB_UNROLL) == 0
    return pl.pallas_call(
        _sb_prompt_body,
        grid=(b, n_pairs, l // tq),
        in_specs=[
            pl.BlockSpec(memory_space=pltpu.SMEM),
            pl.BlockSpec((None, tq, LANES), lambda bi, p, i: (bi, i, p)),
            pl.BlockSpec((None, l, LANES), lambda bi, p, i: (bi, 0, p)),
            pl.BlockSpec((None, l, LANES), lambda bi, p, i: (bi, 0, p)),
        ],
        out_specs=pl.BlockSpec((None, tq, LANES), lambda bi, p, i: (bi, i, p)),
        out_shape=jax.ShapeDtypeStruct((b, l, d), BF16),
        scratch_shapes=[pltpu.VMEM((tq, LANES), F32), pltpu.VMEM((2, tq, LANES), F32)],
        compiler_params=_cparams(3, 48),
        name="sb_prompt",
    )(b_sb, q, k, v)


def _sb_sample_body(pt_ref, q_ref, kn_ref, vn_ref, bias_ref, *rest, pages_per_step):
    del pt_ref
    k_refs = rest[:pages_per_step]
    v_refs = rest[pages_per_step:2 * pages_per_step]
    o_ref, qblk_scr, acc_scr, r_scr = rest[2 * pages_per_step:]
    s = pl.program_id(1)
    ls, d = q_ref.shape
    hd = d // H_C
    kb = LANES
    u_mat = _later_matrix(kb)
    row_head = lax.broadcasted_iota(jnp.int32, (kb, d), 0) // ls
    col_head = lax.broadcasted_iota(jnp.int32, (kb, d), 1) // hd
    own = row_head == col_head
    bias = bias_ref[...] * LOG2E

    @pl.when(s == 0)
    def _():
        q_tiled = jnp.concatenate([q_ref[...]] * H_C, axis=0)
        qblk = jnp.where(own, q_tiled, 0.0).astype(BF16)
        qblk_scr[...] = qblk
        pad = jnp.zeros((kb - ls, d), F32)
        kn = jnp.concatenate([kn_ref[...], pad], axis=0).astype(BF16)
        vn = jnp.concatenate([vn_ref[...], pad], axis=0).astype(BF16)
        z2 = lax.dot_general(qblk, kn, _NT, preferred_element_type=F32) + bias
        qi = lax.broadcasted_iota(jnp.int32, (kb, kb), 0) % ls
        kj = lax.broadcasted_iota(jnp.int32, (kb, kb), 1)
        mask = kj < qi
        sps, laters = _sb_softplus_cumsum([z2], u_mat, mask)
        weights, r_new = _sb_weights([z2], sps, laters, jnp.zeros((kb, LANES), F32), mask)
        r_scr[...] = r_new
        acc_scr[...] = _dot(weights[0].astype(BF16), vn)

    k_cat = jnp.concatenate([k_refs[p][...].astype(BF16) for p in range(pages_per_step)], axis=1)
    v_cat = jnp.concatenate([v_refs[p][...].astype(BF16) for p in range(pages_per_step)], axis=1)
    z2_all = _dot(qblk_scr[...], k_cat)
    z2s = [z2_all[:, p * kb:(p + 1) * kb] + bias for p in range(pages_per_step)]
    sps, laters = _sb_softplus_cumsum(z2s, u_mat, None)
    weights, r_new = _sb_weights(z2s, sps, laters, r_scr[...], None)
    r_scr[...] = r_new
    acc_scr[...] += lax.dot_general(jnp.concatenate(weights, axis=1).astype(BF16), v_cat, _NT,
                                    preferred_element_type=F32)

    @pl.when(s == pl.num_programs(1) - 1)
    def _():
        acc = jnp.where(own, acc_scr[...], 0.0)
        out = acc[0:ls, :]
        for hh in range(1, H_C):
            out = out + acc[hh * ls:(hh + 1) * ls, :]
        o_ref[...] = out.astype(o_ref.dtype)


def _sb_sample(q, k_new, v_new, bias_rows, cache_kt, cache_vt, page_table, *, pages_per_step):
    bs, ls, d = q.shape
    n_pages = page_table.shape[1]
    page = cache_kt.shape[-1]
    n_steps = n_pages // pages_per_step
    kb = LANES

    def page_spec(r):
        return pl.BlockSpec((None, d, page), lambda b, s, pt: (pt[b, n_pages - 1 - (s * pages_per_step + r)], 0, 0))

    seq = lambda: pl.BlockSpec((None, ls, d), lambda b, s, pt: (b, 0, 0))
    grid_spec = pltpu.PrefetchScalarGridSpec(
        num_scalar_prefetch=1,
        grid=(bs, n_steps),
        in_specs=[seq(), seq(), seq(), pl.BlockSpec((kb, kb), lambda b, s, pt: (0, 0))]
        + [page_spec(r) for r in range(pages_per_step)] * 2,
        out_specs=seq(),
        scratch_shapes=[pltpu.VMEM((kb, d), BF16), pltpu.VMEM((kb, d), F32), pltpu.VMEM((kb, LANES), F32)],
    )
    return pl.pallas_call(
        functools.partial(_sb_sample_body, pages_per_step=pages_per_step),
        grid_spec=grid_spec,
        out_shape=jax.ShapeDtypeStruct((bs, ls, d), BF16),
        compiler_params=_cparams(2, 56),
        name="sb_sample",
    )(page_table, q, k_new, v_new, bias_rows, *([cache_kt] * pages_per_step), *([cache_vt] * pages_per_step))


def _out_proj_body(x_ref, o_ref_in, w_ref, y_ref):
    y_ref[...] = x_ref[...] + _dot(o_ref_in[...], w_ref[...].astype(BF16))


def _out_proj(x, o, w, idx, *, tm):
    t, d = x.shape
    return pl.pallas_call(
        _out_proj_body,
        grid=(t // tm,),
        in_specs=[
            pl.BlockSpec((tm, d), lambda i: (i, 0)),
            pl.BlockSpec((tm, d), lambda i: (i, 0)),
            pl.BlockSpec((None, d, d), lambda i: (idx, 0, 0)),
        ],
        out_specs=pl.BlockSpec((tm, d), lambda i: (i, 0)),
        out_shape=jax.ShapeDtypeStruct((t, d), F32),
        compiler_params=_cparams(1, 40),
        name="sb_out_proj",
    )(x, o, w)


def kernel(x_prompt, x_sample, mem_prompt, cache_conv, cache_sb_k, cache_sb_v, cache_mem_k, cache_mem_v, page_table, g_ffn1, f1_w_gu, f1_w_d, g_mix, a_w_in, a_b_in, a_ln_g, a_ln_b, a_w_s, a_b_s, a_w_out, b_w_pw1, b_b_pw1, b_w_dw, b_b_dw, b_ln_g, b_ln_b, b_w_pw2, b_b_pw2, c_w_qkv, c_b_sb, c_w_o, g_mem_q, g_mem_kv, m_w_q, m_w_kv, m_w_o, g_ffn2, f2_w_gu, f2_w_d, g_final):
    bp, lp, d = x_prompt.shape
    bs, ls, _ = x_sample.shape
    depth = g_ffn1.shape[0]
    n_mem = mem_prompt.shape[1]
    d_a = a_w_out.shape[1]
    tp, ts = bp * lp, bs * ls
    page = cache_sb_k.shape[2]
    seqs_per_chunk = CHUNK // ls

    row3 = lambda a: a[:, None, :]
    g1, gm, gq, gkv, g2 = (row3(a) for a in (g_ffn1, g_mix, g_mem_q, g_mem_kv, g_ffn2))
    gf = g_final[None, :]

    kv_mem = _mem_kv(mem_prompt.reshape(bp * n_mem, d), gkv, m_w_kv)
    dk = kv_mem.shape[-1] // 2
    mk_p = kv_mem[..., :dk].reshape(depth, bp, n_mem, dk)
    mv_p = kv_mem[..., dk:].reshape(depth, bp, n_mem, dk)
    mk_s = cache_mem_k.reshape(depth, bs, n_mem * MEM_H, dk // MEM_H)
    mv_s = cache_mem_v.reshape(depth, bs, n_mem * MEM_H, dk // MEM_H)

    eye = jnp.eye(seqs_per_chunk, dtype=F32)
    a_w_in_b = a_w_in.astype(BF16)
    a_w_out_b = a_w_out.astype(BF16)
    w1_b = b_w_pw1.astype(BF16)
    w2_b = b_w_pw2.astype(BF16)
    wqkv_b = c_w_qkv.astype(BF16)

    yp = x_prompt.reshape(tp, d)
    ys = x_sample.reshape(ts, d)
    chunk_v_p, chunk_v_s, conv_p, conv_s = [], [], [], []
    k_p, v_p, k_s, v_s = [], [], [], []

    for l in range(depth):
        kind, i = l % 3, l // 3
        attn_p = {}
        ys, *w_bf16 = _ffn_cast(ys, g1, f1_w_gu, f1_w_d, l)
        yp = _ffn(yp, g1, *w_bf16, l, tm=TM_FFN)
        if kind == 0:
            bias_p = a_b_s[i][:, :, None]
            yp3, vrows_p = _chunk_mlp(yp.reshape(bp, lp, d), gm, a_w_in_b, row3(a_b_in), row3(a_ln_g), row3(a_ln_b),
                                      a_w_s[i], bias_p, a_w_out_b, l, i, tm=TM_CHUNK, v_rows=CHUNK)
            w_s_blk = jnp.einsum('ab,hts->hatbs', eye, a_w_s[i][:, :ls, :ls]).reshape(H_A, CHUNK, CHUNK)
            bias_s = jnp.tile(a_b_s[i][:, :ls], (1, seqs_per_chunk))[:, :, None]
            ys3, vrows_s = _chunk_mlp(ys.reshape(1, ts, d), gm, a_w_in_b, row3(a_b_in), row3(a_ln_g), row3(a_ln_b),
                                      w_s_blk, bias_s, a_w_out_b, l, i, tm=ts, v_rows=ts)
            yp, ys = yp3.reshape(tp, d), ys3.reshape(ts, d)
            chunk_v_p.append(vrows_p)
            chunk_v_s.append(vrows_s.reshape(bs, ls, d_a))
        elif kind == 1:
            conv_args = (gm, w1_b, row3(b_b_pw1), b_w_dw, row3(b_b_dw), row3(b_ln_g), row3(b_ln_b), w2_b, row3(b_b_pw2))
            zeros_hist = jnp.zeros((bp, CONV_W - 1, d), F32)
            yp3, hist_p = _conv_module(yp.reshape(bp, lp, d), zeros_hist, *conv_args, l, i, tm=TM_CONV, stride=1)
            ys_t = ys.reshape(bs, ls, d).transpose(1, 0, 2).reshape(1, ts, d)
            hist_t = cache_conv[i].transpose(1, 0, 2).reshape(1, (CONV_W - 1) * bs, d)
            ys_t, hist_s = _conv_module(ys_t, hist_t, *conv_args, l, i, tm=ts, stride=bs)
            yp = yp3.reshape(tp, d)
            ys = ys_t.reshape(ls, bs, d).transpose(1, 0, 2).reshape(ts, d)
            conv_p.append(hist_p)
            conv_s.append(hist_s.reshape(CONV_W - 1, bs, d).transpose(1, 0, 2))
        else:
            qp, kp_t, vp_t, kpb, vpb = _qkv_proj(yp, gm, wqkv_b, l, i, tm=TM_QKV, seq_len=lp)
            shp = (bp, lp, d)
            op = _sb_prompt(qp.reshape(shp), kpb.reshape(shp), vpb.reshape(shp), c_b_sb[i][None, :], tq=TQ_SB)
            attn_p = dict(attn=op, w_attn=c_w_o, attn_idx=i)
            qs, ks_new, vs_new, _, _ = _qkv_proj(ys, gm, wqkv_b, l, i, tm=ts)
            kt = cache_sb_k[i].transpose(0, 2, 3, 1).reshape(-1, d, page)
            vt = cache_sb_v[i].transpose(0, 2, 3, 1).reshape(-1, d, page)
            bias_rows = jnp.broadcast_to(jnp.repeat(c_b_sb[i], ls)[:, None], (H_C * ls, LANES))
            shs = (bs, ls, d)
            os_ = _sb_sample(qs.reshape(shs).astype(F32), ks_new.reshape(shs), vs_new.reshape(shs), bias_rows,
                             kt, vt, page_table, pages_per_step=SB_PAGES_PER_STEP)
            ys = _out_proj(ys, os_.reshape(ts, d), c_w_o, i, tm=ts)
            hd = d // H_C
            k_p.append(kp_t.reshape(bp, H_C, hd, lp).transpose(0, 3, 1, 2))
            v_p.append(vp_t.reshape(bp, H_C, hd, lp).transpose(0, 3, 1, 2))
            k_s.append(ks_new.reshape(bs, ls, H_C, hd))
            v_s.append(vs_new.reshape(bs, ls, H_C, hd))
        yp = _mem_attn(yp.reshape(bp, lp, d), gq, m_w_q, m_w_o, mk_p, mv_p, l, tm=TM_MEM, kv_layer=l,
                       **attn_p).reshape(tp, d)
        ys = _mem_attn_rows(ys, gq, m_w_q, m_w_o, mk_s, mv_s, l, ls=ls)
        last = l == depth - 1
        ys, *w_bf16 = _ffn_cast(ys, g2, f2_w_gu, f2_w_d, l, g_final=gf if last else None)
        yp = _ffn(yp, g2, *w_bf16, l, tm=TM_FFN, g_final=gf if last else None)

    mem_hd = dk // MEM_H
    return (yp.reshape(bp, lp, d), ys.reshape(bs, ls, d),
            jnp.stack(chunk_v_p), jnp.stack(chunk_v_s), jnp.stack(conv_p), jnp.stack(conv_s),
            jnp.stack(k_p), jnp.stack(v_p), jnp.stack(k_s), jnp.stack(v_s),
            mk_p.reshape(depth, bp, n_mem, MEM_H, mem_hd), mv_p.reshape(depth, bp, n_mem, MEM_H, mem_hd))
```
